```python
import jax
import jax.numpy as jnp
from jax import lax
import numpy as np

D_MODEL = 2048
BATCH = 1
SEQ = 8192
DEPTH = 4

GRID_W = 64
CTX_LEN = 256

N_EVEN = (DEPTH + 1) // 2
N_ODD = DEPTH // 2
N_MOD = 6
NORM_EPS = 1e-6
ROPE_THETA = 10000.0
Q_BLOCK = 128
NEG_INF = -1e30

A_HEAD_DIM = 128
A_HEADS = (D_MODEL // 2) // A_HEAD_DIM
A_KV_HEADS = A_HEADS // 4
A_GROUP = A_HEADS // A_KV_HEADS
A_Q = A_HEADS * A_HEAD_DIM
A_KV = A_KV_HEADS * A_HEAD_DIM

B_WIDTH = D_MODEL // 2
B_HEADS = 8
B_BLOCK = B_WIDTH // B_HEADS
CONV_W = 4
CONV_LEFT = CONV_W // 2
RG_C = 8.0

EVEN_SPLITS = [A_Q, A_Q + A_KV, A_Q + 2 * A_KV, A_Q + 2 * A_KV + B_WIDTH]
EVEN_IN = A_Q + 2 * A_KV + 2 * B_WIDTH
EVEN_MIX = A_Q + B_WIDTH

C_HEAD_DIM = 64
C_HEADS = D_MODEL // C_HEAD_DIM
C_KV_HEADS = C_HEADS // 8
C_GROUP = C_HEADS // C_KV_HEADS
C_Q = C_HEADS * C_HEAD_DIM
C_KV = C_KV_HEADS * C_HEAD_DIM
WINDOW = 128
ODD_SPLITS = [C_Q, C_Q + C_KV]
ODD_IN = C_Q + 2 * C_KV

N_EXPERTS = 64
TOP_K = 8
D_EXPERT = 384
D_SHARED = 384
ROUTED_SCALE = 2.5
MOE_BLOCK = 128

kernel_name = 'hybrid_diffusion_gqa_rglru_swa_moe'


def rmsnorm(x, g):
    xf = x.astype(jnp.float32)
    y = xf * lax.rsqrt(jnp.mean(xf * xf, axis=-1, keepdims=True) + NORM_EPS)
    return (y * g.astype(jnp.float32)).astype(x.dtype)


def modulate(x, g, shift, scale):
    return rmsnorm(x, g) * (1 + scale) + shift


def ada_params(cond, w, b):
    return jnp.split(jax.nn.silu(cond) @ w + b, N_MOD, axis=-1)


def axial_rope(n_tok, head_dim, dtype):
    n_rows = n_tok // GRID_W
    rows = jnp.repeat(jnp.arange(n_rows, dtype=jnp.float32), GRID_W)
    cols = jnp.tile(jnp.arange(GRID_W, dtype=jnp.float32), n_rows)
    d_axis = head_dim // 2
    inv = ROPE_THETA ** (-jnp.arange(0, d_axis, 2, dtype=jnp.float32) / d_axis)
    ar = rows[:, None] * inv
    ac = cols[:, None] * inv
    ang = jnp.concatenate([ar, ar, ac, ac], axis=-1)
    return (jnp.cos(ang).astype(dtype)[None, :, None, :],
            jnp.sin(ang).astype(dtype)[None, :, None, :])


def rotate_axial(x):
    x1, x2, x3, x4 = jnp.split(x, 4, axis=-1)
    return jnp.concatenate([-x2, x1, -x4, x3], axis=-1)


def apply_rope(x, cos, sin):
    return x * cos + rotate_axial(x) * sin


def attend(q, k, v, sink=None, mask=None):
    hkv, g, dh = q.shape[2], q.shape[3], q.shape[4]
    s = jnp.einsum('bqkgd,bskd->bkgqs', q, k).astype(jnp.float32) * (dh ** -0.5)
    if mask is not None:
        s = jnp.where(mask, s, NEG_INF)
    if sink is None:
        p = jax.nn.softmax(s, axis=-1)
    else:
        sk = sink.astype(jnp.float32).reshape(hkv, g)[None, :, :, None, None]
        m = jnp.maximum(jnp.max(s, axis=-1, keepdims=True), sk)
        e = jnp.exp(s - m)
        p = e / (jnp.sum(e, axis=-1, keepdims=True) + jnp.exp(sk - m))
    return jnp.einsum('bkgqs,bskd->bqkgd', p.astype(v.dtype), v)


def dense_attend_blocks(q, k, v):
    b, s, hq, dh = q.shape
    hkv = k.shape[2]
    nb = s // Q_BLOCK
    qb = jnp.moveaxis(q.reshape(b, nb, Q_BLOCK, hkv, hq // hkv, dh), 1, 0)
    o = lax.map(lambda qblk: attend(qblk, k, v), qb)
    return jnp.moveaxis(o, 0, 1).reshape(b, s, hq * dh)


def window_attend_blocks(q, k, v, k_ctx, v_ctx, sink):
    b, s, hq, dh = q.shape
    hkv = k.shape[2]
    nb = s // Q_BLOCK
    span = Q_BLOCK + 2 * WINDOW
    n_ctx = k_ctx.shape[1]
    kp = jnp.pad(k, ((0, 0), (WINDOW, WINDOW), (0, 0), (0, 0)))
    vp = jnp.pad(v, ((0, 0), (WINDOW, WINDOW), (0, 0), (0, 0)))
    qb = jnp.moveaxis(q.reshape(b, nb, Q_BLOCK, hkv, hq // hkv, dh), 1, 0)
    rel = jnp.arange(span)[None, :] - jnp.arange(Q_BLOCK)[:, None]
    band = (rel >= 0) & (rel <= 2 * WINDOW)
    ctx_ok = jnp.ones((Q_BLOCK, n_ctx), dtype=bool)

    def one(args):
        qblk, blk = args
        start = blk * Q_BLOCK
        kb = lax.dynamic_slice_in_dim(kp, start, span, axis=1)
        vb = lax.dynamic_slice_in_dim(vp, start, span, axis=1)
        pos = start - WINDOW + jnp.arange(span)
        ok = band & ((pos >= 0) & (pos < s))[None, :]
        keys = jnp.concatenate([k_ctx, kb], axis=1)
        vals = jnp.concatenate([v_ctx, vb], axis=1)
        mask = jnp.concatenate([ctx_ok, ok], axis=1)
        return attend(qblk, keys, vals, sink=sink, mask=mask)

    o = lax.map(one, (qb, jnp.arange(nb)))
    return jnp.moveaxis(o, 0, 1).reshape(b, s, hq * dh)


def short_conv(x, w, bias):
    s = x.shape[1]
    xp = jnp.pad(x, ((0, 0), (CONV_LEFT, CONV_W - 1 - CONV_LEFT), (0, 0)))
    out = xp[:, 0:s] * w[0]
    for j in range(1, CONV_W):
        out = out + xp[:, j:j + s] * w[j]
    return out + bias


def rglru_coeffs(xc, rgate_w, rgate_b, igate_w, igate_b, lam):
    xh = xc.reshape(*xc.shape[:-1], B_HEADS, B_BLOCK)

    def gate(w, bias):
        z = jnp.einsum('bshi,rhij->rbshj', xh, w).reshape(2, *xc.shape) + bias[:, None, None, :]
        return jax.nn.sigmoid(z.astype(jnp.float32))

    r = gate(rgate_w, rgate_b)
    i = gate(igate_w, igate_b)
    log_a = RG_C * r * jax.nn.log_sigmoid(lam.astype(jnp.float32))[:, None, None, :]
    a = jnp.exp(log_a)
    u = jnp.sqrt(-jnp.expm1(2.0 * log_a)) * (i * xc.astype(jnp.float32))
    return a, u


def linear_scan(a, u, h0):
    def combine(left, right):
        a_l, b_l = left
        a_r, b_r = right
        return a_l * a_r, a_r * b_l + b_r
    a_cum, b_cum = lax.associative_scan(combine, (a, u), axis=1)
    return a_cum * h0[:, None, :] + b_cum


def even_mixer(h_ctx, h_lat, w_in, w_out, q_norm, k_norm, conv_w, conv_b,
               rgate_w, rgate_b, igate_w, igate_b, lam, cos, sin, need_ctx):
    b, n_ctx = h_ctx.shape[0], h_ctx.shape[1]

    def project(h):
        bs = h.shape[:2]
        q, k, v, xr, gr = jnp.split(h @ w_in, EVEN_SPLITS, axis=-1)
        q = rmsnorm(q.reshape(*bs, A_HEADS, A_HEAD_DIM), q_norm)
        k = rmsnorm(k.reshape(*bs, A_KV_HEADS, A_HEAD_DIM), k_norm)
        v = v.reshape(*bs, A_KV_HEADS, A_HEAD_DIM)
        a, u = rglru_coeffs(short_conv(xr, conv_w, conv_b), rgate_w, rgate_b, igate_w, igate_b, lam)
        return q, k, v, a, u, gr

    q_c, k_c, v_c, a_c, u_c, g_c = project(h_ctx)
    q_l, k_l, v_l, a_l, u_l, g_l = project(h_lat)
    q_l = apply_rope(q_l, cos, sin)
    k_l = apply_rope(k_l, cos, sin)
    att_l = dense_attend_blocks(q_l, jnp.concatenate([k_c, k_l], axis=1),
                                jnp.concatenate([v_c, v_l], axis=1))
    h0 = jnp.zeros((b, B_WIDTH), jnp.float32)
    hf_c = linear_scan(a_c[0], u_c[0], h0)
    hb_c = linear_scan(a_c[1][:, ::-1], u_c[1][:, ::-1], h0)
    hf_l = linear_scan(a_l[0], u_l[0], hf_c[:, -1])
    hb_l = linear_scan(a_l[1][:, ::-1], u_l[1][:, ::-1], hb_c[:, -1])[:, ::-1]
    rec_l = (hf_l + hb_l).astype(h_lat.dtype) * jax.nn.gelu(g_l)
    out_l = jnp.concatenate([att_l, rec_l], axis=-1) @ w_out
    if not need_ctx:
        return None, out_l
    att_c = attend(q_c.reshape(b, n_ctx, A_KV_HEADS, A_GROUP, A_HEAD_DIM), k_c, v_c).reshape(b, n_ctx, A_Q)
    rec_c = (hf_c + hb_c[:, ::-1]).astype(h_ctx.dtype) * jax.nn.gelu(g_c)
    out_c = jnp.concatenate([att_c, rec_c], axis=-1) @ w_out
    return out_c, out_l


def odd_mixer(h_ctx, h_lat, w_in, w_out, sink, cos, sin, need_ctx):
    b, n_ctx = h_ctx.shape[0], h_ctx.shape[1]

    def project(h):
        bs = h.shape[:2]
        q, k, v = jnp.split(h @ w_in, ODD_SPLITS, axis=-1)
        return (q.reshape(*bs, C_HEADS, C_HEAD_DIM),
                k.reshape(*bs, C_KV_HEADS, C_HEAD_DIM),
                v.reshape(*bs, C_KV_HEADS, C_HEAD_DIM))

    q_c, k_c, v_c = project(h_ctx)
    q_l, k_l, v_l = project(h_lat)
    q_l = apply_rope(q_l, cos, sin)
    k_l = apply_rope(k_l, cos, sin)
    out_l = window_attend_blocks(q_l, k_l, v_l, k_c, v_c, sink) @ w_out
    if not need_ctx:
        return None, out_l
    att_c = attend(q_c.reshape(b, n_ctx, C_KV_HEADS, C_GROUP, C_HEAD_DIM), k_c, v_c,
                   sink=sink).reshape(b, n_ctx, C_Q)
    return att_c @ w_out, out_l


def moe_ffn(h, router_w, router_b, w_gate, w_up, w_down, sh_gate, sh_up, sh_down):
    b, n, d = h.shape
    n_tok = b * n
    tok = h.reshape(n_tok, d)
    scores = jax.nn.sigmoid((tok @ router_w).astype(jnp.float32))
    _, top_idx = lax.top_k(scores + router_b.astype(jnp.float32), TOP_K)
    top_s = jnp.take_along_axis(scores, top_idx, axis=-1)
    gates = ROUTED_SCALE * top_s / jnp.sum(top_s, axis=-1, keepdims=True)
    n_assign = n_tok * TOP_K
    flat_e = top_idx.reshape(n_assign)
    flat_tok = jnp.arange(n_assign, dtype=jnp.int32) // TOP_K
    order = jnp.argsort(flat_e)
    e_sorted = flat_e[order]
    counts = jnp.bincount(flat_e, length=N_EXPERTS)
    padded = (counts + MOE_BLOCK - 1) // MOE_BLOCK * MOE_BLOCK
    pad_end = jnp.cumsum(padded)
    pad_start = pad_end - padded
    cnt_start = jnp.cumsum(counts) - counts
    dest = pad_start[e_sorted] + jnp.arange(n_assign) - cnt_start[e_sorted]
    n_rows = -(-(n_assign + N_EXPERTS * (MOE_BLOCK - 1)) // MOE_BLOCK) * MOE_BLOCK
    n_blocks = n_rows // MOE_BLOCK
    row_tok = jnp.zeros((n_rows,), jnp.int32).at[dest].set(flat_tok[order])
    row_gate = jnp.zeros((n_rows,), jnp.float32).at[dest].set(gates.reshape(n_assign)[order])
    block_e = jnp.minimum(jnp.searchsorted(pad_end, jnp.arange(n_blocks) * MOE_BLOCK, side='right'),
                          N_EXPERTS - 1)

    def expert_block(args):
        idx, e = args
        xb = tok[idx]
        return (jax.nn.silu(xb @ w_gate[e]) * (xb @ w_up[e])) @ w_down[e]

    y = lax.map(expert_block, (row_tok.reshape(n_blocks, MOE_BLOCK), block_e))
    y = y.reshape(n_rows, d) * row_gate[:, None].astype(h.dtype)
    routed = jax.ops.segment_sum(y, row_tok, num_segments=n_tok)
    shared = (jax.nn.silu(tok @ sh_gate) * (tok @ sh_up)) @ sh_down
    return (routed + shared).reshape(b, n, d)


def setup_inputs(seed: int = 0) -> dict:
    key = jax.random.key(seed)
    ks = jax.random.split(key, 31)
    f32 = jnp.float32

    def nrm(i, shape, scale):
        return jax.random.normal(ks[i], shape, f32) * scale

    u = jax.random.uniform(ks[19], (N_EVEN, 2, B_WIDTH), f32, 0.9, 0.999)
    a_base = u ** (1.0 / RG_C)
    b_lambda = jnp.log(a_base) - jnp.log1p(-a_base)
    return {
        'x': nrm(0, (BATCH, SEQ, D_MODEL), 1.0),
        'c': nrm(1, (BATCH, D_MODEL), 1.0),
        'ctx': nrm(2, (BATCH, CTX_LEN, D_MODEL), 1.0),
        'c_ctx': nrm(3, (D_MODEL,), 1.0),
        'w_mod': nrm(4, (DEPTH, D_MODEL, N_MOD * D_MODEL), 0.5 * D_MODEL ** -0.5),
        'b_mod': nrm(5, (DEPTH, N_MOD * D_MODEL), 0.02),
        'norm1_g': 1.0 + nrm(6, (DEPTH, D_MODEL), 0.02),
        'norm2_g': 1.0 + nrm(7, (DEPTH, D_MODEL), 0.02),
        'final_norm_g': 1.0 + nrm(8, (D_MODEL,), 0.02),
        'a_w_in': nrm(9, (N_EVEN, D_MODEL, EVEN_IN), D_MODEL ** -0.5),
        'a_w_out': nrm(10, (N_EVEN, EVEN_MIX, D_MODEL), EVEN_MIX ** -0.5),
        'a_q_norm': 1.0 + nrm(11, (N_EVEN, A_HEAD_DIM), 0.02),
        'a_k_norm': 1.0 + nrm(12, (N_EVEN, A_HEAD_DIM), 0.02),
        'b_conv_w': nrm(13, (N_EVEN, CONV_W, B_WIDTH), CONV_W ** -0.5),
        'b_conv_b': nrm(14, (N_EVEN, B_WIDTH), 0.01),
        'b_rgate_w': nrm(15, (N_EVEN, 2, B_HEADS, B_BLOCK, B_BLOCK), B_BLOCK ** -0.5),
        'b_rgate_b': nrm(16, (N_EVEN, 2, B_WIDTH), 0.01),
        'b_igate_w': nrm(17, (N_EVEN, 2, B_HEADS, B_BLOCK, B_BLOCK), B_BLOCK ** -0.5),
        'b_igate_b': nrm(18, (N_EVEN, 2, B_WIDTH), 0.01),
        'b_lambda': b_lambda,
        'c_w_in': nrm(20, (N_ODD, D_MODEL, ODD_IN), D_MODEL ** -0.5),
        'c_w_out': nrm(21, (N_ODD, C_Q, D_MODEL), C_Q ** -0.5),
        'c_sink': nrm(22, (N_ODD, C_HEADS), 1.0),
        'moe_router_w': nrm(23, (DEPTH, D_MODEL, N_EXPERTS), D_MODEL ** -0.5),
        'moe_router_b': nrm(24, (DEPTH, N_EXPERTS), 0.01),
        'moe_w_gate': nrm(25, (DEPTH, N_EXPERTS, D_MODEL, D_EXPERT), D_MODEL ** -0.5),
        'moe_w_up': nrm(26, (DEPTH, N_EXPERTS, D_MODEL, D_EXPERT), D_MODEL ** -0.5),
        'moe_w_down': nrm(27, (DEPTH, N_EXPERTS, D_EXPERT, D_MODEL), D_EXPERT ** -0.5),
        'moe_shared_gate': nrm(28, (DEPTH, D_MODEL, D_SHARED), D_MODEL ** -0.5),
        'moe_shared_up': nrm(29, (DEPTH, D_MODEL, D_SHARED), D_MODEL ** -0.5),
        'moe_shared_down': nrm(30, (DEPTH, D_SHARED, D_MODEL), D_SHARED ** -0.5),
    }


def reference(x, c, ctx, c_ctx, w_mod, b_mod, norm1_g, norm2_g, final_norm_g,
              a_w_in, a_w_out, a_q_norm, a_k_norm, b_conv_w, b_conv_b,
              b_rgate_w, b_rgate_b, b_igate_w, b_igate_b, b_lambda,
              c_w_in, c_w_out, c_sink,
              moe_router_w, moe_router_b, moe_w_gate, moe_w_up, moe_w_down,
              moe_shared_gate, moe_shared_up, moe_shared_down):
    n_lat = x.shape[1]
    n_ctx = ctx.shape[1]
    cos_a, sin_a = axial_rope(n_lat, A_HEAD_DIM, x.dtype)
    cos_c, sin_c = axial_rope(n_lat, C_HEAD_DIM, x.dtype)
    x_lat, x_ctx = x, ctx
    for layer in range(DEPTH):
        need_ctx = layer < DEPTH - 1
        sh1, sc1, g1, sh2, sc2, g2 = [m[:, None, :] for m in ada_params(c, w_mod[layer], b_mod[layer])]
        csh1, csc1, cg1, csh2, csc2, cg2 = ada_params(c_ctx, w_mod[layer], b_mod[layer])
        h_lat = modulate(x_lat, norm1_g[layer], sh1, sc1)
        h_ctx = modulate(x_ctx, norm1_g[layer], csh1, csc1)
        i = layer // 2
        if layer % 2 == 0:
            o_ctx, o_lat = even_mixer(h_ctx, h_lat, a_w_in[i], a_w_out[i], a_q_norm[i], a_k_norm[i],
                                      b_conv_w[i], b_conv_b[i], b_rgate_w[i], b_rgate_b[i],
                                      b_igate_w[i], b_igate_b[i], b_lambda[i], cos_a, sin_a, need_ctx)
        else:
            o_ctx, o_lat = odd_mixer(h_ctx, h_lat, c_w_in[i], c_w_out[i], c_sink[i],
                                     cos_c, sin_c, need_ctx)
        x_lat = x_lat + g1 * o_lat
        moe_args = (moe_router_w[layer], moe_router_b[layer], moe_w_gate[layer], moe_w_up[layer],
                    moe_w_down[layer], moe_shared_gate[layer], moe_shared_up[layer],
                    moe_shared_down[layer])
        if need_ctx:
            x_ctx = x_ctx + cg1 * o_ctx
            h2 = jnp.concatenate([modulate(x_ctx, norm2_g[layer], csh2, csc2),
                                  modulate(x_lat, norm2_g[layer], sh2, sc2)], axis=1)
            f = moe_ffn(h2, *moe_args)
            x_ctx = x_ctx + cg2 * f[:, :n_ctx]
            x_lat = x_lat + g2 * f[:, n_ctx:]
        else:
            x_lat = x_lat + g2 * moe_ffn(modulate(x_lat, norm2_g[layer], sh2, sc2), *moe_args)
    return rmsnorm(x_lat, final_norm_g)
```

```python
import functools

import jax
import jax.numpy as jnp
from jax import lax
from jax.experimental import pallas as pl
from jax.experimental.pallas import tpu as pltpu

F32 = jnp.float32
BF16 = jnp.bfloat16
I32 = jnp.int32

NORM_EPS = 1e-6
ROPE_THETA = 10000.0
GRID_W = 64
N_MOD = 6
A_HEAD_DIM = 128
A_GROUP = 4
B_HEADS = 8
CONV_W = 4
RG_C = 8.0
C_HEAD_DIM = 64
C_GROUP = 8
WINDOW = 128
TOP_K = 8
ROUTED_SCALE = 2.5
NEG_INF = -1e30

LANES = 128
SUBLANES = 8
ROW_TILE = 256
ATT_Q_TILE = 256
ATT_KV_CHUNK = 1024
WIN_Q_TILE = 128
EXPERT_ROWS = 256
COMBINE_TILE = 128
ADA_COLS = 512
VMEM_LIMIT = 48 * 1024 * 1024


def _params(*sem):
    return pltpu.CompilerParams(dimension_semantics=sem, vmem_limit_bytes=VMEM_LIMIT)


def _modulate(xf, g, shift, scale):
    ms = jnp.mean(xf * xf, axis=-1, keepdims=True)
    y = xf * lax.rsqrt(ms + NORM_EPS) * g
    return y * (1.0 + scale) + shift


def _silu(x):
    return x * jax.nn.sigmoid(x)


def _dot(a, b):
    return jnp.dot(a, b, preferred_element_type=F32)


def _dot_nt(a, b):
    return lax.dot_general(a, b, (((1,), (1,)), ((), ())), preferred_element_type=F32)


def _ada_kernel(cond_ref, w_ref, b_ref, o_ref):
    tn = w_ref.shape[2]
    for cnd in range(2):
        s = _silu(cond_ref[cnd])
        for j in range(tn // LANES):
            cols = slice(j * LANES, (j + 1) * LANES)
            acc = jnp.sum(w_ref[0, :, cols] * s, axis=0, keepdims=True)
            o_ref[0, cnd:cnd + 1, cols] = acc + b_ref[0, :, cols]


def _ada_params(cond, w_mod, b_mod):
    depth, d, n = w_mod.shape
    cond_b = jnp.broadcast_to(cond[:, :, None], (2, d, LANES))
    return pl.pallas_call(
        _ada_kernel,
        grid=(depth, n // ADA_COLS),
        in_specs=[pl.BlockSpec((2, d, LANES), lambda l, j: (0, 0, 0)),
                  pl.BlockSpec((1, d, ADA_COLS), lambda l, j: (l, 0, j)),
                  pl.BlockSpec((1, 1, ADA_COLS), lambda l, j: (l, 0, j))],
        out_specs=pl.BlockSpec((1, 2, ADA_COLS), lambda l, j: (l, 0, j)),
        out_shape=jax.ShapeDtypeStruct((depth, 2, n), F32),
        name="ada_params",
        compiler_params=_params("parallel", "parallel"),
    )(cond_b, w_mod, b_mod.reshape(depth, 1, n))


def _rope_tables(n_lat, n_ctx, head_dim):
    n_rows = n_lat // GRID_W
    rows = jnp.repeat(jnp.arange(n_rows, dtype=F32), GRID_W)
    cols = jnp.tile(jnp.arange(GRID_W, dtype=F32), n_rows)
    d_axis = head_dim // 2
    inv = ROPE_THETA ** (-jnp.arange(0, d_axis, 2, dtype=F32) / d_axis)
    ar = rows[:, None] * inv
    ac = cols[:, None] * inv
    ang = jnp.concatenate([ar, ar, ac, ac], axis=-1)
    ang = jnp.tile(ang, (1, LANES // head_dim))
    chunk = head_dim // 4
    even = (jnp.arange(LANES) // chunk) % 2 == 0
    cos, sin = jnp.cos(ang), jnp.sin(ang)
    sa = jnp.where(even, -sin, 0.0)
    sb = jnp.where(even, 0.0, sin)
    pad = ((0, n_ctx), (0, 0))
    return (jnp.pad(cos, pad, constant_values=1.0), jnp.pad(sa, pad), jnp.pad(sb, pad))


def _rope(y, cos, sa, sb, chunk):
    return y * cos + pltpu.roll(y, LANES - chunk, 1) * sa + pltpu.roll(y, chunk, 1) * sb


def _in_proj_even_kernel(x_ref, g_ref, mod_ref, w_ref, cos_ref, sa_ref, sb_ref, qn_ref, kn_ref,
                         q_ref, k_ref, v_ref, xr_ref, gr_ref):
    h = _modulate(x_ref[...], g_ref[...], mod_ref[0, 0:1, :], mod_ref[0, 1:2, :])
    z = _dot(h.astype(BF16), w_ref[...])
    cos, sa, sb = cos_ref[...], sa_ref[...], sb_ref[...]
    a_q, a_kv, b_w = q_ref.shape[1], k_ref.shape[1], xr_ref.shape[1]

    def norm_rope(zh, gain):
        ms = jnp.mean(zh * zh, axis=-1, keepdims=True)
        return _rope(zh * lax.rsqrt(ms + NORM_EPS) * gain, cos, sa, sb, A_HEAD_DIM // 4)

    scale = A_HEAD_DIM ** -0.5
    for hd in range(a_q // LANES):
        cols = slice(hd * LANES, (hd + 1) * LANES)
        q_ref[:, cols] = (norm_rope(z[:, cols], qn_ref[...]) * scale).astype(BF16)
    for hd in range(a_kv // LANES):
        cols = slice(hd * LANES, (hd + 1) * LANES)
        k_ref[:, cols] = norm_rope(z[:, a_q + hd * LANES:a_q + (hd + 1) * LANES], kn_ref[...]).astype(BF16)
    v_ref[...] = z[:, a_q + a_kv:a_q + 2 * a_kv].astype(BF16)
    xr_ref[...] = z[:, a_q + 2 * a_kv:a_q + 2 * a_kv + b_w]
    gr_ref[...] = z[:, a_q + 2 * a_kv + b_w:]


def _row_spec(tm, n):
    return pl.BlockSpec((tm, n), lambda i: (i, 0))


def _full_spec(shape):
    nd = len(shape)
    return pl.BlockSpec(shape, lambda i: (0,) * nd)


def _mod_spec(d, n_lat_tiles):
    return pl.BlockSpec((1, N_MOD, d), lambda i: (jnp.where(i >= n_lat_tiles, 1, 0), 0, 0))


def _in_proj_even(x, g, mod, w, tables, qn, kn, n_lat):
    m, d = x.shape
    tm = ROW_TILE
    a_q = d // 2
    a_kv = a_q // A_GROUP
    b_w = d // 2
    cos, sa, sb = tables
    return pl.pallas_call(
        _in_proj_even_kernel,
        grid=(m // tm,),
        in_specs=[_row_spec(tm, d), _full_spec((1, d)), _mod_spec(d, n_lat // tm), _full_spec(w.shape),
                  _row_spec(tm, LANES), _row_spec(tm, LANES), _row_spec(tm, LANES),
                  _full_spec((1, LANES)), _full_spec((1, LANES))],
        out_specs=[_row_spec(tm, a_q), _row_spec(tm, a_kv), _row_spec(tm, a_kv),
                   _row_spec(tm, b_w), _row_spec(tm, b_w)],
        out_shape=[jax.ShapeDtypeStruct((m, a_q), BF16), jax.ShapeDtypeStruct((m, a_kv), BF16),
                   jax.ShapeDtypeStruct((m, a_kv), BF16), jax.ShapeDtypeStruct((m, b_w), F32),
                   jax.ShapeDtypeStruct((m, b_w), F32)],
        name="in_proj_even",
        compiler_params=_params("parallel"),
    )(x, g.reshape(1, d), mod, w, cos, sa, sb, qn.reshape(1, LANES), kn.reshape(1, LANES))


def _in_proj_odd_kernel(x_ref, g_ref, mod_ref, w_ref, cos_ref, sa_ref, sb_ref,
                        q_ref, klo_ref, khi_ref, vlo_ref, vhi_ref):
    h = _modulate(x_ref[...], g_ref[...], mod_ref[0, 0:1, :], mod_ref[0, 1:2, :])
    z = _dot(h.astype(BF16), w_ref[...])
    cos, sa, sb = cos_ref[...], sa_ref[...], sb_ref[...]
    tm = x_ref.shape[0]
    c_q = q_ref.shape[1]
    c_kv = klo_ref.shape[1] // 2
    scale = C_HEAD_DIM ** -0.5
    for j in range(c_q // LANES):
        cols = slice(j * LANES, (j + 1) * LANES)
        q_ref[:, cols] = (_rope(z[:, cols], cos, sa, sb, C_HEAD_DIM // 4) * scale).astype(BF16)
    lo = lax.broadcasted_iota(I32, (tm, LANES), 1) < C_HEAD_DIM

    def expand(pair, lo_ref, hi_ref, j):
        swapped = pltpu.roll(pair, C_HEAD_DIM, 1)
        c0 = slice(2 * j * LANES, (2 * j + 1) * LANES)
        c1 = slice((2 * j + 1) * LANES, (2 * j + 2) * LANES)
        lo_ref[:, c0] = jnp.where(lo, pair, 0.0).astype(BF16)
        hi_ref[:, c0] = jnp.where(lo, 0.0, swapped).astype(BF16)
        lo_ref[:, c1] = jnp.where(lo, swapped, 0.0).astype(BF16)
        hi_ref[:, c1] = jnp.where(lo, 0.0, pair).astype(BF16)

    for j in range(c_kv // LANES):
        k0 = c_q + j * LANES
        v0 = c_q + c_kv + j * LANES
        expand(_rope(z[:, k0:k0 + LANES], cos, sa, sb, C_HEAD_DIM // 4), klo_ref, khi_ref, j)
        expand(z[:, v0:v0 + LANES], vlo_ref, vhi_ref, j)


def _in_proj_odd(x, g, mod, w, tables, n_lat):
    m, d = x.shape
    tm = ROW_TILE
    c_q = d
    c_kv = d // C_GROUP
    cos, sa, sb = tables
    kv_shape = jax.ShapeDtypeStruct((m, 2 * c_kv), BF16)
    return pl.pallas_call(
        _in_proj_odd_kernel,
        grid=(m // tm,),
        in_specs=[_row_spec(tm, d), _full_spec((1, d)), _mod_spec(d, n_lat // tm), _full_spec(w.shape),
                  _row_spec(tm, LANES), _row_spec(tm, LANES), _row_spec(tm, LANES)],
        out_specs=[_row_spec(tm, c_q)] + [_row_spec(tm, 2 * c_kv)] * 4,
        out_shape=[jax.ShapeDtypeStruct((m, c_q), BF16), kv_shape, kv_shape, kv_shape, kv_shape],
        name="in_proj_odd",
        compiler_params=_params("parallel"),
    )(x, g.reshape(1, d), mod, w, cos, sa, sb)


def _dense_attn_kernel(q_ref, k_ref, v_ref, o_ref, m_ref, l_ref, acc_ref, *, n_lat, n_ctx, tk):
    tq = q_ref.shape[0]
    is_lat = pl.program_id(1) < n_lat // tq
    q = jnp.concatenate([q_ref[:, g * LANES:(g + 1) * LANES] for g in range(A_GROUP)], axis=0)
    m_ref[...] = jnp.full(m_ref.shape, NEG_INF, F32)
    l_ref[...] = jnp.zeros(l_ref.shape, F32)
    acc_ref[...] = jnp.zeros(acc_ref.shape, F32)

    def step(kc, vc):
        s = _dot_nt(q, kc)
        m_old = m_ref[...]
        m_new = jnp.maximum(m_old, jnp.max(s, axis=-1, keepdims=True))
        alpha = jnp.exp(m_old - m_new)
        p = jnp.exp(s - m_new)
        l_ref[...] = alpha * l_ref[...] + jnp.sum(p, axis=-1, keepdims=True)
        acc_ref[...] = alpha * acc_ref[...] + _dot(p.astype(BF16), vc)
        m_ref[...] = m_new

    @pl.when(is_lat)
    def _():
        def body(i, carry):
            start = pl.multiple_of(i * tk, tk)
            step(k_ref[pl.ds(start, tk), :], v_ref[pl.ds(start, tk), :])
            return carry
        lax.fori_loop(0, n_lat // tk, body, 0)

    step(k_ref[pl.ds(n_lat, n_ctx), :], v_ref[pl.ds(n_lat, n_ctx), :])
    out = acc_ref[...] / l_ref[...]
    for g in range(A_GROUP):
        o_ref[:, g * LANES:(g + 1) * LANES] = out[g * tq:(g + 1) * tq].astype(BF16)


def _dense_attention(q, k, v, n_lat, tk):
    m, a_q = q.shape
    n_kv = k.shape[1] // LANES
    tq = ATT_Q_TILE
    gw = A_GROUP * LANES
    rows = A_GROUP * tq
    return pl.pallas_call(
        functools.partial(_dense_attn_kernel, n_lat=n_lat, n_ctx=m - n_lat, tk=tk),
        grid=(n_kv, m // tq),
        in_specs=[pl.BlockSpec((tq, gw), lambda kh, i: (i, kh)),
                  pl.BlockSpec((m, LANES), lambda kh, i: (0, kh)),
                  pl.BlockSpec((m, LANES), lambda kh, i: (0, kh))],
        out_specs=pl.BlockSpec((tq, gw), lambda kh, i: (i, kh)),
        out_shape=jax.ShapeDtypeStruct((m, a_q), BF16),
        scratch_shapes=[pltpu.VMEM((rows, 1), F32), pltpu.VMEM((rows, 1), F32),
                        pltpu.VMEM((rows, LANES), F32)],
        name="dense_attention",
        compiler_params=_params("parallel", "parallel"),
    )(q, k, v)


def _window_attn_kernel(sink_ref, q_ref, klo_ref, khi_ref, vlo_ref, vhi_ref, o_ref, *, n_lat, n_ctx):
    tq = q_ref.shape[0]
    n_pairs = C_GROUP // 2
    span = tq + 2 * WINDOW
    g = pl.program_id(0)
    b = pl.program_id(1)
    is_lat = b < n_lat // tq
    ws = pl.multiple_of(jnp.clip((b - 1) * tq, 0, n_lat - span), tq)
    off = jnp.where(is_lat, ws - b * tq, 4 * span)
    qp = jnp.concatenate([q_ref[:, j * LANES:(j + 1) * LANES] for j in range(n_pairs)], axis=0)
    rel = (lax.broadcasted_iota(I32, (tq, span), 1) - lax.broadcasted_iota(I32, (tq, span), 0)) + off
    valid = jnp.concatenate([jnp.abs(rel) <= WINDOW] * n_pairs, axis=0)

    def half(k_ref, v_ref, first_head):
        kc, vc = k_ref[pl.ds(n_lat, n_ctx), :], v_ref[pl.ds(n_lat, n_ctx), :]
        kw, vw = k_ref[pl.ds(ws, span), :], v_ref[pl.ds(ws, span), :]
        s_c = _dot_nt(qp, kc)
        s_w = jnp.where(valid, _dot_nt(qp, kw), NEG_INF)
        sk = jnp.concatenate([jnp.full((tq, 1), sink_ref[g * C_GROUP + 2 * j + first_head], F32)
                              for j in range(n_pairs)], axis=0)
        mx = jnp.maximum(jnp.maximum(jnp.max(s_c, axis=-1, keepdims=True),
                                     jnp.max(s_w, axis=-1, keepdims=True)), sk)
        e_c = jnp.exp(s_c - mx)
        e_w = jnp.exp(s_w - mx)
        den = (jnp.sum(e_c, axis=-1, keepdims=True) + jnp.sum(e_w, axis=-1, keepdims=True)
               + jnp.exp(sk - mx))
        return _dot(e_c.astype(BF16), vc) + _dot(e_w.astype(BF16), vw), den

    o_lo, d_lo = half(klo_ref, vlo_ref, 0)
    o_hi, d_hi = half(khi_ref, vhi_ref, 1)
    lo = lax.broadcasted_iota(I32, o_lo.shape, 1) < C_HEAD_DIM
    out = jnp.where(lo, o_lo / d_lo, o_hi / d_hi)
    for j in range(n_pairs):
        o_ref[:, j * LANES:(j + 1) * LANES] = out[j * tq:(j + 1) * tq].astype(BF16)


def _window_attention(q, klo, khi, vlo, vhi, sink, n_lat):
    m, c_q = q.shape
    n_kv = klo.shape[1] // LANES
    tq = WIN_Q_TILE
    gw = (C_GROUP // 2) * LANES
    kv_spec = pl.BlockSpec((m, LANES), lambda g, b, s: (0, g))
    return pl.pallas_call(
        functools.partial(_window_attn_kernel, n_lat=n_lat, n_ctx=m - n_lat),
        grid_spec=pltpu.PrefetchScalarGridSpec(
            num_scalar_prefetch=1,
            grid=(n_kv, m // tq),
            in_specs=[pl.BlockSpec((tq, gw), lambda g, b, s: (b, g)), kv_spec, kv_spec, kv_spec, kv_spec],
            out_specs=pl.BlockSpec((tq, gw), lambda g, b, s: (b, g))),
        out_shape=jax.ShapeDtypeStruct((m, c_q), BF16),
        name="window_attention",
        compiler_params=_params("parallel", "parallel"),
    )(sink, q, klo, khi, vlo, vhi)


def _rglru_kernel(*refs, reverse, n_tiles, n_lat_tiles):
    if reverse:
        (xp_ref, x_ref, xn_ref, cw_ref, cb_ref, w_ref, rb_ref, ib_ref, lam_ref, hf_ref, gr_ref,
         out_ref, xe_scr, a_scr, u_scr, h_scr, hb_scr) = refs
    else:
        (xp_ref, x_ref, xn_ref, cw_ref, cb_ref, w_ref, rb_ref, ib_ref, lam_ref,
         out_ref, xe_scr, a_scr, u_scr, h_scr) = refs
    tm, bw = x_ref.shape
    pid = pl.program_id(0)
    tile = (n_tiles - 1 - pid) if reverse else (pid + n_lat_tiles) % n_tiles
    seq_start = (tile == 0) | (tile == n_lat_tiles)
    seq_end = (tile == n_lat_tiles - 1) | (tile == n_tiles - 1)

    @pl.when(pid == 0)
    def _():
        h_scr[...] = jnp.zeros(h_scr.shape, F32)

    xe_scr[0:SUBLANES, :] = jnp.where(seq_start, 0.0, xp_ref[...])
    xe_scr[SUBLANES:SUBLANES + tm, :] = x_ref[...]
    xe_scr[SUBLANES + tm:, :] = jnp.where(seq_end, 0.0, xn_ref[...])
    left = CONV_W // 2
    xc = xe_scr[SUBLANES - left:SUBLANES - left + tm, :] * cw_ref[0:1, :]
    for j in range(1, CONV_W):
        s0 = SUBLANES - left + j
        xc = xc + xe_scr[s0:s0 + tm, :] * cw_ref[j:j + 1, :]
    xc = xc + cb_ref[...]

    blk = bw // B_HEADS
    for hd in range(B_HEADS):
        cols = slice(hd * blk, (hd + 1) * blk)
        xh = xc[:, cols]
        zz = _dot(xh.astype(BF16), w_ref[0, hd])
        r = jax.nn.sigmoid(zz[:, :blk] + rb_ref[0, :, cols])
        gi = jax.nn.sigmoid(zz[:, blk:] + ib_ref[0, :, cols])
        lam = lam_ref[0, :, cols]
        log_sig = -(jnp.maximum(-lam, 0.0) + jnp.log1p(jnp.exp(-jnp.abs(lam))))
        log_a = RG_C * r * log_sig
        th = jnp.tanh(log_a)
        a_scr[:, cols] = jnp.exp(log_a)
        u_scr[:, cols] = jnp.sqrt(-2.0 * th / (1.0 - th)) * (gi * xh)

    dst = hb_scr if reverse else out_ref

    def body(j, h):
        t = (tm - 1 - j) if reverse else j
        h = a_scr[pl.ds(t, 1), :] * h + u_scr[pl.ds(t, 1), :]
        dst[pl.ds(t, 1), :] = h
        return h

    h_scr[...] = lax.fori_loop(0, tm, body, h_scr[...], unroll=8)

    if reverse:
        gr = gr_ref[...]
        cdf = 0.5 * (1.0 + jnp.tanh(0.7978845608028654 * (gr + 0.044715 * (gr * gr * gr))))
        out_ref[...] = ((hf_ref[...] + hb_scr[...]) * (gr * cdf)).astype(BF16)


def _rglru(xr, gr, conv_w, conv_b, wcat, rgate_b, igate_b, lam, n_lat):
    m, bw = xr.shape
    tm = ROW_TILE
    n_tiles, n_lat_tiles = m // tm, n_lat // tm
    per_tile = tm // SUBLANES
    n_sub = m // SUBLANES
    blk = bw // B_HEADS

    def run(reverse, extra_in):
        d = 1 if reverse else 0
        if reverse:
            tile = lambda i: n_tiles - 1 - i
        else:
            tile = lambda i: (i + n_lat_tiles) % n_tiles
        row = pl.BlockSpec((tm, bw), lambda i: (tile(i), 0))
        in_specs = [pl.BlockSpec((SUBLANES, bw), lambda i: (jnp.maximum(tile(i) * per_tile - 1, 0), 0)),
                    row,
                    pl.BlockSpec((SUBLANES, bw), lambda i: (jnp.minimum((tile(i) + 1) * per_tile, n_sub - 1), 0)),
                    _full_spec((CONV_W, bw)), _full_spec((1, bw)),
                    pl.BlockSpec((1, B_HEADS, blk, 2 * blk), lambda i: (d, 0, 0, 0)),
                    pl.BlockSpec((1, 1, bw), lambda i: (d, 0, 0)),
                    pl.BlockSpec((1, 1, bw), lambda i: (d, 0, 0)),
                    pl.BlockSpec((1, 1, bw), lambda i: (d, 0, 0))] + [row] * len(extra_in)
        scratch = [pltpu.VMEM((tm + 2 * SUBLANES, bw), F32), pltpu.VMEM((tm, bw), F32),
                   pltpu.VMEM((tm, bw), F32), pltpu.VMEM((1, bw), F32)]
        if reverse:
            scratch.append(pltpu.VMEM((tm, bw), F32))
        return pl.pallas_call(
            functools.partial(_rglru_kernel, reverse=reverse, n_tiles=n_tiles, n_lat_tiles=n_lat_tiles),
            grid=(n_tiles,),
            in_specs=in_specs,
            out_specs=row,
            out_shape=jax.ShapeDtypeStruct((m, bw), BF16 if reverse else F32),
            scratch_shapes=scratch,
            name="rglru_reverse" if reverse else "rglru_forward",
            compiler_params=_params("arbitrary"),
        )(xr, xr, xr, conv_w, conv_b.reshape(1, bw), wcat, rgate_b.reshape(2, 1, bw),
          igate_b.reshape(2, 1, bw), lam.reshape(2, 1, bw), *extra_in)

    hf = run(False, ())
    return run(True, (hf, gr))


def _post_mixer_kernel(*refs, n_att):
    att_refs = refs[:n_att]
    wout_refs = refs[n_att:2 * n_att]
    (x_ref, mod_ref, g2_ref, rwh_ref, rwl_ref, rb_ref,
     x1_ref, h2_ref, eidx_ref, rank_ref, gate_ref, cnt_ref, cnt_scr) = refs[2 * n_att:]
    tm = x_ref.shape[0]

    @pl.when(pl.program_id(0) == 0)
    def _():
        cnt_scr[...] = jnp.zeros(cnt_scr.shape, F32)

    o = _dot(att_refs[0][...], wout_refs[0][...])
    for a_ref, w_ref in zip(att_refs[1:], wout_refs[1:]):
        o = o + _dot(a_ref[...], w_ref[...])
    x1 = x_ref[...] + mod_ref[0, 2:3, :] * o
    x1_ref[...] = x1
    h2 = _modulate(x1, g2_ref[...], mod_ref[0, 3:4, :], mod_ref[0, 4:5, :])
    h2_ref[...] = h2

    hh = h2.astype(BF16)
    hl = (h2 - hh.astype(F32)).astype(BF16)
    logits = _dot(hh, rwh_ref[...]) + _dot(hl, rwh_ref[...]) + _dot(hh, rwl_ref[...])
    scores = jax.nn.sigmoid(logits)
    sel = scores + rb_ref[...]
    lane = lax.broadcasted_iota(I32, (tm, LANES), 1)
    picked = jnp.zeros((tm, LANES), jnp.bool_)
    idxs, vals = [], []
    for _ in range(TOP_K):
        mx = jnp.max(sel, axis=-1, keepdims=True)
        idx = jnp.min(jnp.where(sel == mx, lane, LANES), axis=-1, keepdims=True)
        hit = lane == idx
        vals.append(jnp.sum(jnp.where(hit, scores, 0.0), axis=-1, keepdims=True))
        idxs.append(idx)
        sel = jnp.where(hit, -3e38, sel)
        picked = picked | hit
    total = vals[0]
    for v in vals[1:]:
        total = total + v

    pick_f = picked.astype(F32)
    lower = (lax.broadcasted_iota(I32, (tm, tm), 0) > lax.broadcasted_iota(I32, (tm, tm), 1)).astype(BF16)
    rank_dense = _dot(lower, pick_f.astype(BF16)) + cnt_scr[...]
    cnt_scr[...] = cnt_scr[...] + jnp.sum(pick_f, axis=0, keepdims=True)
    cnt_ref[...] = cnt_scr[...]

    eidx = jnp.zeros((tm, LANES), I32)
    rank = jnp.zeros((tm, LANES), I32)
    gate = jnp.zeros((tm, LANES), F32)
    for k in range(TOP_K):
        rk = jnp.sum(jnp.where(lane == idxs[k], rank_dense, 0.0), axis=-1, keepdims=True)
        eidx = jnp.where(lane == k, idxs[k], eidx)
        rank = jnp.where(lane == k, rk.astype(I32), rank)
        gate = jnp.where(lane == k, ROUTED_SCALE * vals[k] / total, gate)
    eidx_ref[...] = eidx
    rank_ref[...] = rank
    gate_ref[...] = gate


def _post_mixer(atts, wouts, x, mod, g2, rwh, rwl, rb, n_lat):
    m, d = x.shape
    tm = ROW_TILE
    n_att = len(atts)
    in_specs = ([_row_spec(tm, a.shape[1]) for a in atts] + [_full_spec(w.shape) for w in wouts]
                + [_row_spec(tm, d), _mod_spec(d, n_lat // tm), _full_spec((1, d)),
                   _full_spec(rwh.shape), _full_spec(rwl.shape), _full_spec((1, LANES))])
    return pl.pallas_call(
        functools.partial(_post_mixer_kernel, n_att=n_att),
        grid=(m // tm,),
        in_specs=in_specs,
        out_specs=[_row_spec(tm, d), _row_spec(tm, d), _row_spec(tm, LANES), _row_spec(tm, LANES),
                   _row_spec(tm, LANES), _full_spec((1, LANES))],
        out_shape=[jax.ShapeDtypeStruct((m, d), F32), jax.ShapeDtypeStruct((m, d), F32),
                   jax.ShapeDtypeStruct((m, LANES), I32), jax.ShapeDtypeStruct((m, LANES), I32),
                   jax.ShapeDtypeStruct((m, LANES), F32), jax.ShapeDtypeStruct((1, LANES), F32)],
        scratch_shapes=[pltpu.VMEM((1, LANES), F32)],
        name="post_mixer",
        compiler_params=_params("arbitrary"),
    )(*atts, *wouts, x, mod, g2.reshape(1, d), rwh, rwl, rb)


def _row_copy(src_hbm, row, dst_vmem, slot, sem):
    return pltpu.make_async_copy(src_hbm.at[pl.ds(row, 1)], dst_vmem.at[pl.ds(slot, 1)], sem)


def _expert_kernel(be_ref, nu_ref, rt_ref, h2_hbm, wg_ref, wu_ref, wd_ref, y_ref,
                   xbuf, wgb, wub, wdb, sem):
    bm = xbuf.shape[0]
    b = pl.program_id(0)
    used = b < nu_ref[0]

    @pl.when(used)
    def _():
        def issue(r, c):
            _row_copy(h2_hbm, rt_ref[0, 0, r], xbuf, r, sem).start()
            return c
        lax.fori_loop(0, bm, issue, 0)

        @pl.when((b == 0) | (be_ref[b] != be_ref[jnp.maximum(b - 1, 0)]))
        def _():
            wgb[...] = wg_ref[0].astype(BF16)
            wub[...] = wu_ref[0].astype(BF16)
            wdb[...] = wd_ref[0].astype(BF16)

        def wait(r, c):
            _row_copy(h2_hbm, 0, xbuf, r, sem).wait()
            return c
        lax.fori_loop(0, bm, wait, 0)

        xb = xbuf[...].astype(BF16)
        hm = _silu(_dot(xb, wgb[...])) * _dot(xb, wub[...])
        y_ref[...] = _dot(hm.astype(BF16), wdb[...])

    @pl.when(jnp.logical_not(used))
    def _():
        y_ref[...] = jnp.zeros(y_ref.shape, F32)


def _experts(h2, row_tok, block_e, n_used, w_gate, w_up, w_down):
    m, d = h2.shape
    n_e, _, d_e = w_gate.shape
    nb, _, bm = row_tok.shape
    return pl.pallas_call(
        _expert_kernel,
        grid_spec=pltpu.PrefetchScalarGridSpec(
            num_scalar_prefetch=2,
            grid=(nb,),
            in_specs=[pl.BlockSpec((1, 1, bm), lambda b, be, nu: (b, 0, 0), memory_space=pltpu.SMEM),
                      pl.BlockSpec(memory_space=pl.ANY),
                      pl.BlockSpec((1, d, d_e), lambda b, be, nu: (be[b], 0, 0)),
                      pl.BlockSpec((1, d, d_e), lambda b, be, nu: (be[b], 0, 0)),
                      pl.BlockSpec((1, d_e, d), lambda b, be, nu: (be[b], 0, 0))],
            out_specs=pl.BlockSpec((bm, d), lambda b, be, nu: (b, 0)),
            scratch_shapes=[pltpu.VMEM((bm, d), F32), pltpu.VMEM((d, d_e), BF16),
                            pltpu.VMEM((d, d_e), BF16), pltpu.VMEM((d_e, d), BF16),
                            pltpu.SemaphoreType.DMA]),
        out_shape=jax.ShapeDtypeStruct((nb * bm, d), F32),
        name="experts",
        compiler_params=_params("arbitrary"),
    )(block_e, n_used, row_tok, h2, w_gate, w_up, w_down)


def _combine_kernel(*refs, final):
    if final:
        (dest_ref, y_hbm, x1_ref, h2_ref, gate_ref, mod_ref, sg_ref, su_ref, sd_ref, fg_ref,
         o_ref, ybuf, sem) = refs
    else:
        (dest_ref, y_hbm, x1_ref, h2_ref, gate_ref, mod_ref, sg_ref, su_ref, sd_ref,
         o_ref, ybuf, sem) = refs
    tc = x1_ref.shape[0]

    def issue(r, c):
        for k in range(TOP_K):
            _row_copy(y_hbm, dest_ref[0, 0, r * TOP_K + k], ybuf.at[k], r, sem).start()
        return c
    lax.fori_loop(0, tc, issue, 0)

    hb = h2_ref[...].astype(BF16)
    shared = _dot((_silu(_dot(hb, sg_ref[...])) * _dot(hb, su_ref[...])).astype(BF16), sd_ref[...])

    def wait(r, c):
        for k in range(TOP_K):
            _row_copy(y_hbm, 0, ybuf.at[k], r, sem).wait()
        return c
    lax.fori_loop(0, tc, wait, 0)

    gate = gate_ref[...]
    routed = gate[:, 0:1] * ybuf[0]
    for k in range(1, TOP_K):
        routed = routed + gate[:, k:k + 1] * ybuf[k]
    x2 = x1_ref[...] + mod_ref[0, 5:6, :] * (routed + shared)
    if final:
        ms = jnp.mean(x2 * x2, axis=-1, keepdims=True)
        x2 = x2 * lax.rsqrt(ms + NORM_EPS) * fg_ref[...]
    o_ref[...] = x2


def _combine(dest, y, x1, h2, gate, mod, sg, su, sd, final_g, n_lat):
    m, d = x1.shape
    tc = COMBINE_TILE
    final = final_g is not None
    in_specs = [pl.BlockSpec((1, 1, tc * TOP_K), lambda i: (i, 0, 0), memory_space=pltpu.SMEM),
                pl.BlockSpec(memory_space=pl.ANY),
                _row_spec(tc, d), _row_spec(tc, d), _row_spec(tc, LANES), _mod_spec(d, n_lat // tc),
                _full_spec(sg.shape), _full_spec(su.shape), _full_spec(sd.shape)]
    args = [dest.reshape(m // tc, 1, tc * TOP_K), y, x1, h2, gate, mod, sg, su, sd]
    if final:
        in_specs.append(_full_spec((1, d)))
        args.append(final_g.reshape(1, d))
    return pl.pallas_call(
        functools.partial(_combine_kernel, final=final),
        grid=(m // tc,),
        in_specs=in_specs,
        out_specs=_row_spec(tc, d),
        out_shape=jax.ShapeDtypeStruct((m, d), F32),
        scratch_shapes=[pltpu.VMEM((TOP_K, tc, d), F32), pltpu.SemaphoreType.DMA],
        name="combine",
        compiler_params=_params("arbitrary"),
    )(*args)


def _routing_tables(eidx, rank, counts, n_experts, bm):
    m = eidx.shape[0]
    nb = -(-(m * TOP_K + n_experts * (bm - 1)) // bm)
    padded = (counts + bm - 1) // bm * bm
    pad_end = jnp.cumsum(padded)
    pad_start = pad_end - padded
    dest = pad_start[eidx] + rank
    tok = jnp.broadcast_to(jnp.arange(m, dtype=I32)[:, None], (m, TOP_K))
    row_tok = jnp.zeros((nb * bm,), I32).at[dest.reshape(-1)].set(tok.reshape(-1))
    block_e = jnp.minimum(jnp.searchsorted(pad_end, jnp.arange(nb, dtype=I32) * bm, side='right'),
                          n_experts - 1).astype(I32)
    n_used = (pad_end[-1:] // bm).astype(I32)
    return dest.astype(I32), row_tok.reshape(nb, 1, bm), block_e, n_used


def kernel(x, c, ctx, c_ctx, w_mod, b_mod, norm1_g, norm2_g, final_norm_g, a_w_in, a_w_out, a_q_norm,
           a_k_norm, b_conv_w, b_conv_b, b_rgate_w, b_rgate_b, b_igate_w, b_igate_b, b_lambda, c_w_in,
           c_w_out, c_sink, moe_router_w, moe_router_b, moe_w_gate, moe_w_up, moe_w_down,
           moe_shared_gate, moe_shared_up, moe_shared_down):
    batch, n_lat, d = x.shape
    n_ctx = ctx.shape[1]
    depth = w_mod.shape[0]
    n_experts = moe_router_w.shape[2]
    assert batch == 1 and n_ctx == ROW_TILE and n_lat % ROW_TILE == 0
    assert n_lat >= WIN_Q_TILE + 2 * WINDOW and n_experts <= LANES

    tk = min(ATT_KV_CHUNK, n_lat)
    xs = jnp.concatenate([x[0], ctx[0]], axis=0)
    mods = _ada_params(jnp.stack([c[0], c_ctx]), w_mod, b_mod).reshape(depth, 2, N_MOD, d)
    tables_a = _rope_tables(n_lat, n_ctx, A_HEAD_DIM)
    tables_c = _rope_tables(n_lat, n_ctx, C_HEAD_DIM)

    pad_e = LANES - n_experts
    rw = jnp.pad(moe_router_w, ((0, 0), (0, 0), (0, pad_e)))
    rwh = rw.astype(BF16)
    rwl = (rw - rwh.astype(F32)).astype(BF16)
    rb = jnp.pad(moe_router_b, ((0, 0), (0, pad_e)), constant_values=NEG_INF).reshape(depth, 1, LANES)

    for layer in range(depth):
        i = layer // 2
        mod = mods[layer]
        if layer % 2 == 0:
            q, k, v, xr, gr = _in_proj_even(xs, norm1_g[layer], mod, a_w_in[i].astype(BF16), tables_a,
                                            a_q_norm[i], a_k_norm[i], n_lat)
            att = _dense_attention(q, k, v, n_lat, tk)
            wcat = jnp.concatenate([b_rgate_w[i], b_igate_w[i]], axis=-1).astype(BF16)
            rec = _rglru(xr, gr, b_conv_w[i], b_conv_b[i], wcat, b_rgate_b[i], b_igate_b[i],
                         b_lambda[i], n_lat)
            w_out = a_w_out[i].astype(BF16)
            a_q = att.shape[1]
            atts, wouts = (att, rec), (w_out[:a_q], w_out[a_q:])
        else:
            q, klo, khi, vlo, vhi = _in_proj_odd(xs, norm1_g[layer], mod, c_w_in[i].astype(BF16),
                                                 tables_c, n_lat)
            att = _window_attention(q, klo, khi, vlo, vhi, c_sink[i], n_lat)
            atts, wouts = (att,), (c_w_out[i].astype(BF16),)
        x1, h2, eidx, rank, gate, cnt = _post_mixer(atts, wouts, xs, mod, norm2_g[layer],
                                                    rwh[layer], rwl[layer], rb[layer], n_lat)
        counts = cnt[0, :n_experts].astype(I32)
        dest, row_tok, block_e, n_used = _routing_tables(eidx[:, :TOP_K], rank[:, :TOP_K], counts,
                                                         n_experts, EXPERT_ROWS)
        y = _experts(h2, row_tok, block_e, n_used, moe_w_gate[layer], moe_w_up[layer], moe_w_down[layer])
        xs = _combine(dest, y, x1, h2, gate, mod, moe_shared_gate[layer].astype(BF16),
                      moe_shared_up[layer].astype(BF16), moe_shared_down[layer].astype(BF16),
                      final_norm_g if layer == depth - 1 else None, n_lat)
    return xs[:n_lat].reshape(batch, n_lat, d)
```

```python
import functools

import jax
import jax.numpy as jnp
from jax import lax
from jax.experimental import pallas as pl
from jax.experimental.pallas import tpu as pltpu

F32 = jnp.float32
BF16 = jnp.bfloat16
I32 = jnp.int32

NORM_EPS = 1e-6
ROPE_THETA = 10000.0
GRID_W = 64
N_MOD = 6
A_HEAD_DIM = 128
A_GROUP = 4
B_HEADS = 8
CONV_W = 4
RG_C = 8.0
C_HEAD_DIM = 64
C_GROUP = 8
WINDOW = 128
TOP_K = 8
ROUTED_SCALE = 2.5
NEG_INF = -1e30
LOG2_E = 1.4426950408889634

LANES = 128
SUBLANES = 8
ROW_TILE = 256
ATT_Q_TILE = 256
ATT_KV_CHUNK = 1024
WIN_Q_TILE = 128
EXPERT_ROWS = 256
COMBINE_TILE = 128
ADA_COLS = 512
VMEM_LIMIT = 48 * 1024 * 1024


def _params(*sem, **kw):
    return pltpu.CompilerParams(dimension_semantics=sem, vmem_limit_bytes=VMEM_LIMIT, **kw)


def _modulate(xf, g, shift, scale):
    ms = jnp.mean(xf * xf, axis=-1, keepdims=True)
    y = xf * lax.rsqrt(ms + NORM_EPS) * g
    return y * (1.0 + scale) + shift


def _silu(x):
    return x * jax.nn.sigmoid(x)


def _dot(a, b):
    return jnp.dot(a, b, preferred_element_type=F32)


def _dot_nt(a, b):
    return lax.dot_general(a, b, (((1,), (1,)), ((), ())), preferred_element_type=F32)


def _ada_kernel(cond_ref, w_ref, b_ref, o_ref):
    tn = w_ref.shape[2]
    for cnd in range(2):
        s = _silu(cond_ref[cnd])
        for j in range(tn // LANES):
            cols = slice(j * LANES, (j + 1) * LANES)
            acc = jnp.sum(w_ref[0, :, cols] * s, axis=0, keepdims=True)
            o_ref[0, cnd:cnd + 1, cols] = acc + b_ref[0, :, cols]


def _ada_params(cond, w_mod, b_mod):
    depth, d, n = w_mod.shape
    cond_b = jnp.broadcast_to(cond[:, :, None], (2, d, LANES))
    return pl.pallas_call(
        _ada_kernel,
        grid=(depth, n // ADA_COLS),
        in_specs=[pl.BlockSpec((2, d, LANES), lambda l, j: (0, 0, 0)),
                  pl.BlockSpec((1, d, ADA_COLS), lambda l, j: (l, 0, j)),
                  pl.BlockSpec((1, 1, ADA_COLS), lambda l, j: (l, 0, j))],
        out_specs=pl.BlockSpec((1, 2, ADA_COLS), lambda l, j: (l, 0, j)),
        out_shape=jax.ShapeDtypeStruct((depth, 2, n), F32),
        name="ada_params",
        compiler_params=_params("parallel", "parallel"),
    )(cond_b, w_mod, b_mod.reshape(depth, 1, n))


def _rope_tables(n_lat, n_ctx, head_dim):
    n_rows = n_lat // GRID_W
    rows = jnp.repeat(jnp.arange(n_rows, dtype=F32), GRID_W)
    cols = jnp.tile(jnp.arange(GRID_W, dtype=F32), n_rows)
    d_axis = head_dim // 2
    inv = ROPE_THETA ** (-jnp.arange(0, d_axis, 2, dtype=F32) / d_axis)
    ar = rows[:, None] * inv
    ac = cols[:, None] * inv
    ang = jnp.concatenate([ar, ar, ac, ac], axis=-1)
    ang = jnp.tile(ang, (1, LANES // head_dim))
    chunk = head_dim // 4
    even = (jnp.arange(LANES) // chunk) % 2 == 0
    cos, sin = jnp.cos(ang), jnp.sin(ang)
    sa = jnp.where(even, -sin, 0.0)
    sb = jnp.where(even, 0.0, sin)
    pad = ((0, n_ctx), (0, 0))
    return (jnp.pad(cos, pad, constant_values=1.0), jnp.pad(sa, pad), jnp.pad(sb, pad))


def _rope(y, cos, sa, sb, chunk):
    return y * cos + pltpu.roll(y, LANES - chunk, 1) * sa + pltpu.roll(y, chunk, 1) * sb


def _in_proj_even_kernel(x_ref, g_ref, mod_ref, w_ref, cos_ref, sa_ref, sb_ref, qn_ref, kn_ref,
                         q_ref, k_ref, v_ref, xr_ref, gr_ref):
    h = _modulate(x_ref[...], g_ref[...], mod_ref[0, 0:1, :], mod_ref[0, 1:2, :])
    z = _dot(h.astype(BF16), w_ref[...])
    cos, sa, sb = cos_ref[...], sa_ref[...], sb_ref[...]
    a_q, a_kv, b_w = q_ref.shape[1], k_ref.shape[1], xr_ref.shape[1]

    def norm_rope(zh, gain):
        ms = jnp.mean(zh * zh, axis=-1, keepdims=True)
        return _rope(zh * lax.rsqrt(ms + NORM_EPS) * gain, cos, sa, sb, A_HEAD_DIM // 4)

    scale = A_HEAD_DIM ** -0.5 * LOG2_E
    for hd in range(a_q // LANES):
        cols = slice(hd * LANES, (hd + 1) * LANES)
        q_ref[:, cols] = (norm_rope(z[:, cols], qn_ref[...]) * scale).astype(BF16)
    for hd in range(a_kv // LANES):
        cols = slice(hd * LANES, (hd + 1) * LANES)
        k_ref[:, cols] = norm_rope(z[:, a_q + hd * LANES:a_q + (hd + 1) * LANES], kn_ref[...]).astype(BF16)
    v_ref[...] = z[:, a_q + a_kv:a_q + 2 * a_kv].astype(BF16)
    xr_ref[...] = z[:, a_q + 2 * a_kv:a_q + 2 * a_kv + b_w]
    gr_ref[...] = z[:, a_q + 2 * a_kv + b_w:]


def _row_spec(tm, n):
    return pl.BlockSpec((tm, n), lambda i: (i, 0))


def _full_spec(shape):
    nd = len(shape)
    return pl.BlockSpec(shape, lambda i: (0,) * nd)


def _mod_spec(d, n_lat_tiles):
    return pl.BlockSpec((1, N_MOD, d), lambda i: (jnp.where(i >= n_lat_tiles, 1, 0), 0, 0))


def _in_proj_even(x, g, mod, w, tables, qn, kn, n_lat):
    m, d = x.shape
    tm = ROW_TILE
    a_q = d // 2
    a_kv = a_q // A_GROUP
    b_w = d // 2
    cos, sa, sb = tables
    return pl.pallas_call(
        _in_proj_even_kernel,
        grid=(m // tm,),
        in_specs=[_row_spec(tm, d), _full_spec((1, d)), _mod_spec(d, n_lat // tm), _full_spec(w.shape),
                  _row_spec(tm, LANES), _row_spec(tm, LANES), _row_spec(tm, LANES),
                  _full_spec((1, LANES)), _full_spec((1, LANES))],
        out_specs=[_row_spec(tm, a_q), _row_spec(tm, a_kv), _row_spec(tm, a_kv),
                   _row_spec(tm, b_w), _row_spec(tm, b_w)],
        out_shape=[jax.ShapeDtypeStruct((m, a_q), BF16), jax.ShapeDtypeStruct((m, a_kv), BF16),
                   jax.ShapeDtypeStruct((m, a_kv), BF16), jax.ShapeDtypeStruct((m, b_w), F32),
                   jax.ShapeDtypeStruct((m, b_w), F32)],
        name="in_proj_even",
        compiler_params=_params("parallel"),
    )(x, g.reshape(1, d), mod, w, cos, sa, sb, qn.reshape(1, LANES), kn.reshape(1, LANES))


def _in_proj_odd_kernel(x_ref, g_ref, mod_ref, w_ref, cos_ref, sa_ref, sb_ref,
                        q_ref, klo_ref, khi_ref, vlo_ref, vhi_ref):
    h = _modulate(x_ref[...], g_ref[...], mod_ref[0, 0:1, :], mod_ref[0, 1:2, :])
    z = _dot(h.astype(BF16), w_ref[...])
    cos, sa, sb = cos_ref[...], sa_ref[...], sb_ref[...]
    tm = x_ref.shape[0]
    c_q = q_ref.shape[1]
    c_kv = klo_ref.shape[1] // 2
    scale = C_HEAD_DIM ** -0.5 * LOG2_E
    for j in range(c_q // LANES):
        cols = slice(j * LANES, (j + 1) * LANES)
        q_ref[:, cols] = (_rope(z[:, cols], cos, sa, sb, C_HEAD_DIM // 4) * scale).astype(BF16)
    lo = lax.broadcasted_iota(I32, (tm, LANES), 1) < C_HEAD_DIM

    def expand(pair, lo_ref, hi_ref, j):
        swapped = pltpu.roll(pair, C_HEAD_DIM, 1)
        c0 = slice(2 * j * LANES, (2 * j + 1) * LANES)
        c1 = slice((2 * j + 1) * LANES, (2 * j + 2) * LANES)
        lo_ref[:, c0] = jnp.where(lo, pair, 0.0).astype(BF16)
        hi_ref[:, c0] = jnp.where(lo, 0.0, swapped).astype(BF16)
        lo_ref[:, c1] = jnp.where(lo, swapped, 0.0).astype(BF16)
        hi_ref[:, c1] = jnp.where(lo, 0.0, pair).astype(BF16)

    for j in range(c_kv // LANES):
        k0 = c_q + j * LANES
        v0 = c_q + c_kv + j * LANES
        expand(_rope(z[:, k0:k0 + LANES], cos, sa, sb, C_HEAD_DIM // 4), klo_ref, khi_ref, j)
        expand(z[:, v0:v0 + LANES], vlo_ref, vhi_ref, j)


def _in_proj_odd(x, g, mod, w, tables, n_lat):
    m, d = x.shape
    tm = ROW_TILE
    c_q = d
    c_kv = d // C_GROUP
    cos, sa, sb = tables
    kv_shape = jax.ShapeDtypeStruct((m, 2 * c_kv), BF16)
    return pl.pallas_call(
        _in_proj_odd_kernel,
        grid=(m // tm,),
        in_specs=[_row_spec(tm, d), _full_spec((1, d)), _mod_spec(d, n_lat // tm), _full_spec(w.shape),
                  _row_spec(tm, LANES), _row_spec(tm, LANES), _row_spec(tm, LANES)],
        out_specs=[_row_spec(tm, c_q)] + [_row_spec(tm, 2 * c_kv)] * 4,
        out_shape=[jax.ShapeDtypeStruct((m, c_q), BF16), kv_shape, kv_shape, kv_shape, kv_shape],
        name="in_proj_odd",
        compiler_params=_params("parallel"),
    )(x, g.reshape(1, d), mod, w, cos, sa, sb)


def _dense_attn_kernel(q_ref, k_ref, v_ref, o_ref, m_ref, l_ref, acc_ref, *, n_lat, n_ctx, tk):
    tq = q_ref.shape[0]
    is_lat = pl.program_id(1) < n_lat // tq
    q = jnp.concatenate([q_ref[:, g * LANES:(g + 1) * LANES] for g in range(A_GROUP)], axis=0)
    m_ref[...] = jnp.full(m_ref.shape, NEG_INF, F32)
    l_ref[...] = jnp.zeros(l_ref.shape, F32)
    acc_ref[...] = jnp.zeros(acc_ref.shape, F32)

    def step(kc, vc):
        s = _dot_nt(q, kc)
        m_old = m_ref[...]
        m_new = jnp.maximum(m_old, jnp.max(s, axis=-1, keepdims=True))
        alpha = jnp.exp2(m_old - m_new)
        p = jnp.exp2(s - m_new)
        l_ref[...] = alpha * l_ref[...] + jnp.sum(p, axis=-1, keepdims=True)
        acc_ref[...] = alpha * acc_ref[...] + _dot(p.astype(BF16), vc)
        m_ref[...] = m_new

    @pl.when(is_lat)
    def _():
        def body(i, carry):
            start = pl.multiple_of(i * tk, tk)
            step(k_ref[pl.ds(start, tk), :], v_ref[pl.ds(start, tk), :])
            return carry
        lax.fori_loop(0, n_lat // tk, body, 0)

    step(k_ref[pl.ds(n_lat, n_ctx), :], v_ref[pl.ds(n_lat, n_ctx), :])
    out = acc_ref[...] / l_ref[...]
    for g in range(A_GROUP):
        o_ref[:, g * LANES:(g + 1) * LANES] = out[g * tq:(g + 1) * tq].astype(BF16)


def _dense_attention(q, k, v, n_lat, tk):
    m, a_q = q.shape
    n_kv = k.shape[1] // LANES
    tq = ATT_Q_TILE
    gw = A_GROUP * LANES
    rows = A_GROUP * tq
    return pl.pallas_call(
        functools.partial(_dense_attn_kernel, n_lat=n_lat, n_ctx=m - n_lat, tk=tk),
        grid=(n_kv, m // tq),
        in_specs=[pl.BlockSpec((tq, gw), lambda kh, i: (i, kh)),
                  pl.BlockSpec((m, LANES), lambda kh, i: (0, kh)),
                  pl.BlockSpec((m, LANES), lambda kh, i: (0, kh))],
        out_specs=pl.BlockSpec((tq, gw), lambda kh, i: (i, kh)),
        out_shape=jax.ShapeDtypeStruct((m, a_q), BF16),
        scratch_shapes=[pltpu.VMEM((rows, 1), F32), pltpu.VMEM((rows, 1), F32),
                        pltpu.VMEM((rows, LANES), F32)],
        name="dense_attention",
        compiler_params=_params("parallel", "parallel"),
    )(q, k, v)


def _window_attn_kernel(sink_ref, q_ref, klo_ref, khi_ref, vlo_ref, vhi_ref, o_ref, *, n_lat, n_ctx):
    tq = q_ref.shape[0]
    n_pairs = C_GROUP // 2
    span = tq + 2 * WINDOW
    g = pl.program_id(0)
    b = pl.program_id(1)
    is_lat = b < n_lat // tq
    ws = pl.multiple_of(jnp.clip((b - 1) * tq, 0, n_lat - span), tq)
    off = jnp.where(is_lat, ws - b * tq, 4 * span)
    qp = jnp.concatenate([q_ref[:, j * LANES:(j + 1) * LANES] for j in range(n_pairs)], axis=0)
    rel = (lax.broadcasted_iota(I32, (tq, span), 1) - lax.broadcasted_iota(I32, (tq, span), 0)) + off
    valid = jnp.concatenate([jnp.abs(rel) <= WINDOW] * n_pairs, axis=0)

    def half(k_ref, v_ref, first_head):
        kc, vc = k_ref[pl.ds(n_lat, n_ctx), :], v_ref[pl.ds(n_lat, n_ctx), :]
        kw, vw = k_ref[pl.ds(ws, span), :], v_ref[pl.ds(ws, span), :]
        s_c = _dot_nt(qp, kc)
        s_w = jnp.where(valid, _dot_nt(qp, kw), NEG_INF)
        sk = jnp.concatenate([jnp.full((tq, 1), sink_ref[g * C_GROUP + 2 * j + first_head] * LOG2_E, F32)
                              for j in range(n_pairs)], axis=0)
        mx = jnp.maximum(jnp.maximum(jnp.max(s_c, axis=-1, keepdims=True),
                                     jnp.max(s_w, axis=-1, keepdims=True)), sk)
        e_c = jnp.exp2(s_c - mx)
        e_w = jnp.exp2(s_w - mx)
        den = (jnp.sum(e_c, axis=-1, keepdims=True) + jnp.sum(e_w, axis=-1, keepdims=True)
               + jnp.exp2(sk - mx))
        return _dot(e_c.astype(BF16), vc) + _dot(e_w.astype(BF16), vw), den

    o_lo, d_lo = half(klo_ref, vlo_ref, 0)
    o_hi, d_hi = half(khi_ref, vhi_ref, 1)
    lo = lax.broadcasted_iota(I32, o_lo.shape, 1) < C_HEAD_DIM
    out = jnp.where(lo, o_lo / d_lo, o_hi / d_hi)
    for j in range(n_pairs):
        o_ref[:, j * LANES:(j + 1) * LANES] = out[j * tq:(j + 1) * tq].astype(BF16)


def _window_attention(q, klo, khi, vlo, vhi, sink, n_lat):
    m, c_q = q.shape
    n_kv = klo.shape[1] // LANES
    tq = WIN_Q_TILE
    gw = (C_GROUP // 2) * LANES
    kv_spec = pl.BlockSpec((m, LANES), lambda g, b, s: (0, g))
    return pl.pallas_call(
        functools.partial(_window_attn_kernel, n_lat=n_lat, n_ctx=m - n_lat),
        grid_spec=pltpu.PrefetchScalarGridSpec(
            num_scalar_prefetch=1,
            grid=(n_kv, m // tq),
            in_specs=[pl.BlockSpec((tq, gw), lambda g, b, s: (b, g)), kv_spec, kv_spec, kv_spec, kv_spec],
            out_specs=pl.BlockSpec((tq, gw), lambda g, b, s: (b, g))),
        out_shape=jax.ShapeDtypeStruct((m, c_q), BF16),
        name="window_attention",
        compiler_params=_params("parallel", "parallel"),
    )(sink, q, klo, khi, vlo, vhi)


def _rglru_kernel(*refs, reverse, n_tiles, n_lat_tiles):
    if reverse:
        (xp_ref, x_ref, xn_ref, cw_ref, cb_ref, w_ref, rb_ref, ib_ref, lam_ref, hf_ref, gr_ref,
         out_ref, xe_scr, a_scr, u_scr, h_scr, hb_scr) = refs
    else:
        (xp_ref, x_ref, xn_ref, cw_ref, cb_ref, w_ref, rb_ref, ib_ref, lam_ref,
         out_ref, xe_scr, a_scr, u_scr, h_scr) = refs
    tm, bw = x_ref.shape
    pid = pl.program_id(0)
    tile = (n_tiles - 1 - pid) if reverse else (pid + n_lat_tiles) % n_tiles
    seq_start = (tile == 0) | (tile == n_lat_tiles)
    seq_end = (tile == n_lat_tiles - 1) | (tile == n_tiles - 1)

    @pl.when(pid == 0)
    def _():
        h_scr[...] = jnp.zeros(h_scr.shape, F32)

    xe_scr[0:SUBLANES, :] = jnp.where(seq_start, 0.0, xp_ref[...])
    xe_scr[SUBLANES:SUBLANES + tm, :] = x_ref[...]
    xe_scr[SUBLANES + tm:, :] = jnp.where(seq_end, 0.0, xn_ref[...])
    left = CONV_W // 2
    xc = xe_scr[SUBLANES - left:SUBLANES - left + tm, :] * cw_ref[0:1, :]
    for j in range(1, CONV_W):
        s0 = SUBLANES - left + j
        xc = xc + xe_scr[s0:s0 + tm, :] * cw_ref[j:j + 1, :]
    xc = xc + cb_ref[...]

    blk = bw // B_HEADS
    for hd in range(B_HEADS):
        cols = slice(hd * blk, (hd + 1) * blk)
        xh = xc[:, cols]
        zz = _dot(xh.astype(BF16), w_ref[0, hd])
        r = jax.nn.sigmoid(zz[:, :blk] + rb_ref[0, :, cols])
        gi = jax.nn.sigmoid(zz[:, blk:] + ib_ref[0, :, cols])
        lam = lam_ref[0, :, cols]
        log_sig = -(jnp.maximum(-lam, 0.0) + jnp.log1p(jnp.exp(-jnp.abs(lam))))
        log_a = RG_C * r * log_sig
        th = jnp.tanh(log_a)
        a_scr[:, cols] = jnp.exp(log_a)
        u_scr[:, cols] = jnp.sqrt(-2.0 * th / (1.0 - th)) * (gi * xh)

    dst = hb_scr if reverse else out_ref

    def body(j, h):
        t = (tm - 1 - j) if reverse else j
        h = a_scr[pl.ds(t, 1), :] * h + u_scr[pl.ds(t, 1), :]
        dst[pl.ds(t, 1), :] = h
        return h

    h_scr[...] = lax.fori_loop(0, tm, body, h_scr[...], unroll=8)

    if reverse:
        gr = gr_ref[...]
        cdf = 0.5 * (1.0 + jnp.tanh(0.7978845608028654 * (gr + 0.044715 * (gr * gr * gr))))
        out_ref[...] = ((hf_ref[...] + hb_scr[...]) * (gr * cdf)).astype(BF16)


def _rglru(xr, gr, conv_w, conv_b, wcat, rgate_b, igate_b, lam, n_lat):
    m, bw = xr.shape
    tm = ROW_TILE
    n_tiles, n_lat_tiles = m // tm, n_lat // tm
    per_tile = tm // SUBLANES
    n_sub = m // SUBLANES
    blk = bw // B_HEADS

    def run(reverse, extra_in):
        d = 1 if reverse else 0
        if reverse:
            tile = lambda i: n_tiles - 1 - i
        else:
            tile = lambda i: (i + n_lat_tiles) % n_tiles
        row = pl.BlockSpec((tm, bw), lambda i: (tile(i), 0))
        in_specs = [pl.BlockSpec((SUBLANES, bw), lambda i: (jnp.maximum(tile(i) * per_tile - 1, 0), 0)),
                    row,
                    pl.BlockSpec((SUBLANES, bw), lambda i: (jnp.minimum((tile(i) + 1) * per_tile, n_sub - 1), 0)),
                    _full_spec((CONV_W, bw)), _full_spec((1, bw)),
                    pl.BlockSpec((1, B_HEADS, blk, 2 * blk), lambda i: (d, 0, 0, 0)),
                    pl.BlockSpec((1, 1, bw), lambda i: (d, 0, 0)),
                    pl.BlockSpec((1, 1, bw), lambda i: (d, 0, 0)),
                    pl.BlockSpec((1, 1, bw), lambda i: (d, 0, 0))] + [row] * len(extra_in)
        scratch = [pltpu.VMEM((tm + 2 * SUBLANES, bw), F32), pltpu.VMEM((tm, bw), F32),
                   pltpu.VMEM((tm, bw), F32), pltpu.VMEM((1, bw), F32)]
        if reverse:
            scratch.append(pltpu.VMEM((tm, bw), F32))
        return pl.pallas_call(
            functools.partial(_rglru_kernel, reverse=reverse, n_tiles=n_tiles, n_lat_tiles=n_lat_tiles),
            grid=(n_tiles,),
            in_specs=in_specs,
            out_specs=row,
            out_shape=jax.ShapeDtypeStruct((m, bw), BF16 if reverse else F32),
            scratch_shapes=scratch,
            name="rglru_reverse" if reverse else "rglru_forward",
            compiler_params=_params("arbitrary"),
        )(xr, xr, xr, conv_w, conv_b.reshape(1, bw), wcat, rgate_b.reshape(2, 1, bw),
          igate_b.reshape(2, 1, bw), lam.reshape(2, 1, bw), *extra_in)

    hf = run(False, ())
    return run(True, (hf, gr))


def _post_mixer_kernel(*refs, n_att):
    att_refs = refs[:n_att]
    wout_refs = refs[n_att:2 * n_att]
    (x_ref, mod_ref, g2_ref, rwh_ref, rwl_ref, rb_ref,
     x1_ref, h2_ref, eidx_ref, rank_ref, gate_ref, cnt_ref, cnt_scr) = refs[2 * n_att:]
    tm = x_ref.shape[0]

    @pl.when(pl.program_id(0) == 0)
    def _():
        cnt_scr[...] = jnp.zeros(cnt_scr.shape, F32)

    o = _dot(att_refs[0][...], wout_refs[0][...])
    for a_ref, w_ref in zip(att_refs[1:], wout_refs[1:]):
        o = o + _dot(a_ref[...], w_ref[...])
    x1 = x_ref[...] + mod_ref[0, 2:3, :] * o
    x1_ref[...] = x1
    h2 = _modulate(x1, g2_ref[...], mod_ref[0, 3:4, :], mod_ref[0, 4:5, :])
    h2_ref[...] = h2

    hh = h2.astype(BF16)
    hl = (h2 - hh.astype(F32)).astype(BF16)
    logits = _dot(hh, rwh_ref[...]) + _dot(hl, rwh_ref[...]) + _dot(hh, rwl_ref[...])
    scores = jax.nn.sigmoid(logits)
    sel = scores + rb_ref[...]
    lane = lax.broadcasted_iota(I32, (tm, LANES), 1)
    picked = jnp.zeros((tm, LANES), jnp.bool_)
    idxs, vals = [], []
    for _ in range(TOP_K):
        mx = jnp.max(sel, axis=-1, keepdims=True)
        idx = jnp.min(jnp.where(sel == mx, lane, LANES), axis=-1, keepdims=True)
        hit = lane == idx
        vals.append(jnp.sum(jnp.where(hit, scores, 0.0), axis=-1, keepdims=True))
        idxs.append(idx)
        sel = jnp.where(hit, -3e38, sel)
        picked = picked | hit
    total = vals[0]
    for v in vals[1:]:
        total = total + v

    pick_f = picked.astype(F32)
    lower = (lax.broadcasted_iota(I32, (tm, tm), 0) > lax.broadcasted_iota(I32, (tm, tm), 1)).astype(BF16)
    rank_dense = _dot(lower, pick_f.astype(BF16)) + cnt_scr[...]
    cnt_scr[...] = cnt_scr[...] + jnp.sum(pick_f, axis=0, keepdims=True)
    cnt_ref[...] = cnt_scr[...]

    eidx = jnp.zeros((tm, LANES), I32)
    rank = jnp.zeros((tm, LANES), I32)
    gate = jnp.zeros((tm, LANES), F32)
    for k in range(TOP_K):
        rk = jnp.sum(jnp.where(lane == idxs[k], rank_dense, 0.0), axis=-1, keepdims=True)
        eidx = jnp.where(lane == k, idxs[k], eidx)
        rank = jnp.where(lane == k, rk.astype(I32), rank)
        gate = jnp.where(lane == k, ROUTED_SCALE * vals[k] / total, gate)
    eidx_ref[...] = eidx
    rank_ref[...] = rank
    gate_ref[...] = gate


def _post_mixer(atts, wouts, x, mod, g2, rwh, rwl, rb, n_lat):
    m, d = x.shape
    tm = ROW_TILE
    n_att = len(atts)
    in_specs = ([_row_spec(tm, a.shape[1]) for a in atts] + [_full_spec(w.shape) for w in wouts]
                + [_row_spec(tm, d), _mod_spec(d, n_lat // tm), _full_spec((1, d)),
                   _full_spec(rwh.shape), _full_spec(rwl.shape), _full_spec((1, LANES))])
    return pl.pallas_call(
        functools.partial(_post_mixer_kernel, n_att=n_att),
        grid=(m // tm,),
        in_specs=in_specs,
        out_specs=[_row_spec(tm, d), _row_spec(tm, d), _row_spec(tm, LANES), _row_spec(tm, LANES),
                   _row_spec(tm, LANES), _full_spec((1, LANES))],
        out_shape=[jax.ShapeDtypeStruct((m, d), F32), jax.ShapeDtypeStruct((m, d), F32),
                   jax.ShapeDtypeStruct((m, LANES), I32), jax.ShapeDtypeStruct((m, LANES), I32),
                   jax.ShapeDtypeStruct((m, LANES), F32), jax.ShapeDtypeStruct((1, LANES), F32)],
        scratch_shapes=[pltpu.VMEM((1, LANES), F32)],
        name="post_mixer",
        compiler_params=_params("arbitrary"),
    )(*atts, *wouts, x, mod, g2.reshape(1, d), rwh, rwl, rb)


def _dispatch_kernel(dest_ref, h2_ref, xs_in, xs_out, sem):
    del xs_in
    tm = h2_ref.shape[0]

    def issue(r, c):
        for k in range(TOP_K):
            pltpu.make_async_copy(h2_ref.at[pl.ds(r, 1)],
                                  xs_out.at[pl.ds(dest_ref[0, 0, r * TOP_K + k], 1)], sem).start()
        return c
    lax.fori_loop(0, tm, issue, 0)
    for _ in range(TOP_K):
        pltpu.make_async_copy(h2_ref, xs_out.at[pl.ds(0, tm)], sem).wait()


def _dispatch(dest, h2, xs):
    m, d = h2.shape
    tm = ROW_TILE
    return pl.pallas_call(
        _dispatch_kernel,
        grid=(m // tm,),
        in_specs=[pl.BlockSpec((1, 1, tm * TOP_K), lambda i: (i, 0, 0), memory_space=pltpu.SMEM),
                  _row_spec(tm, d), pl.BlockSpec(memory_space=pl.ANY)],
        out_specs=pl.BlockSpec(memory_space=pl.ANY),
        out_shape=jax.ShapeDtypeStruct(xs.shape, xs.dtype),
        scratch_shapes=[pltpu.SemaphoreType.DMA],
        input_output_aliases={2: 0},
        name="dispatch",
        compiler_params=_params("arbitrary", disable_bounds_checks=True),
    )(dest.reshape(m // tm, 1, tm * TOP_K), h2, xs)


def _expert_kernel(ps_ref, cnt_ref, xs_hbm, wg_ref, wu_ref, wd_ref, y_in, y_hbm,
                   xbuf, ybuf, wgb, wub, wdb, in_sem, out_sem):
    del y_in
    bm = xbuf.shape[1]
    e = pl.program_id(0)
    base = ps_ref[e]
    nblk = (cnt_ref[e] + bm - 1) // bm

    def rows(j):
        return pl.ds(pl.multiple_of(base + j * bm, bm), bm)

    def in_copy(j, slot):
        return pltpu.make_async_copy(xs_hbm.at[rows(j)], xbuf.at[slot], in_sem.at[slot])

    def out_copy(j, slot):
        return pltpu.make_async_copy(ybuf.at[slot], y_hbm.at[rows(j)], out_sem.at[slot])

    @pl.when(nblk > 0)
    def _():
        in_copy(0, 0).start()

    wgb[...] = wg_ref[0, 0].astype(BF16)
    wub[...] = wu_ref[0, 0].astype(BF16)
    wdb[...] = wd_ref[0, 0].astype(BF16)

    def block(j, slot):
        in_copy(j, slot).wait()

        @pl.when(j + 1 < nblk)
        def _():
            in_copy(j + 1, 1 - slot).start()

        @pl.when(j >= 2)
        def _():
            out_copy(j - 2, slot).wait()

        xb = xbuf[slot].astype(BF16)
        hm = _silu(_dot(xb, wgb[...])) * _dot(xb, wub[...])
        ybuf[slot] = _dot(hm.astype(BF16), wdb[...])
        out_copy(j, slot).start()

    def pair(jj, c):
        block(2 * jj, 0)

        @pl.when(2 * jj + 1 < nblk)
        def _():
            block(2 * jj + 1, 1)
        return c
    lax.fori_loop(0, (nblk + 1) // 2, pair, 0)

    for back in (1, 2):
        @pl.when(nblk >= back)
        def _():
            out_copy(nblk - back, (nblk - back) % 2).wait()


def _experts(xs, y, pad_start, counts, layer, w_gate, w_up, w_down):
    _, n_e, d, d_e = w_gate.shape
    bm = EXPERT_ROWS
    return pl.pallas_call(
        _expert_kernel,
        grid_spec=pltpu.PrefetchScalarGridSpec(
            num_scalar_prefetch=2,
            grid=(n_e,),
            in_specs=[pl.BlockSpec(memory_space=pl.ANY),
                      pl.BlockSpec((1, 1, d, d_e), lambda e, ps, cn: (layer, e, 0, 0)),
                      pl.BlockSpec((1, 1, d, d_e), lambda e, ps, cn: (layer, e, 0, 0)),
                      pl.BlockSpec((1, 1, d_e, d), lambda e, ps, cn: (layer, e, 0, 0)),
                      pl.BlockSpec(memory_space=pl.ANY)],
            out_specs=pl.BlockSpec(memory_space=pl.ANY),
            scratch_shapes=[pltpu.VMEM((2, bm, d), F32), pltpu.VMEM((2, bm, d), F32),
                            pltpu.VMEM((d, d_e), BF16), pltpu.VMEM((d, d_e), BF16),
                            pltpu.VMEM((d_e, d), BF16),
                            pltpu.SemaphoreType.DMA((2,)), pltpu.SemaphoreType.DMA((2,))]),
        out_shape=jax.ShapeDtypeStruct(y.shape, F32),
        input_output_aliases={6: 0},
        name="experts",
        compiler_params=_params("arbitrary"),
    )(pad_start, counts, xs, w_gate, w_up, w_down, y)


def _combine_kernel(*refs, final, n_steps):
    if final:
        (dest_ref, next_dest_ref, y_hbm, x1_ref, h2_ref, gate_ref, mod_ref, sg_ref, su_ref, sd_ref,
         fg_ref, o_ref, ybuf, sem) = refs
    else:
        (dest_ref, next_dest_ref, y_hbm, x1_ref, h2_ref, gate_ref, mod_ref, sg_ref, su_ref, sd_ref,
         o_ref, ybuf, sem) = refs
    tc = x1_ref.shape[0]
    i = pl.program_id(0)
    slot = i % 2

    def gather(d_ref, s):
        def issue(r, c):
            for k in range(TOP_K):
                pltpu.make_async_copy(y_hbm.at[pl.ds(d_ref[0, 0, r * TOP_K + k], 1)],
                                      ybuf.at[s, k, pl.ds(r, 1)], sem.at[s]).start()
            return c
        lax.fori_loop(0, tc, issue, 0)

    @pl.when(i == 0)
    def _():
        gather(dest_ref, 0)

    @pl.when(i + 1 < n_steps)
    def _():
        gather(next_dest_ref, 1 - slot)

    hb = h2_ref[...].astype(BF16)
    shared = _dot((_silu(_dot(hb, sg_ref[...])) * _dot(hb, su_ref[...])).astype(BF16), sd_ref[...])

    for k in range(TOP_K):
        pltpu.make_async_copy(y_hbm.at[pl.ds(0, tc)], ybuf.at[slot, k], sem.at[slot]).wait()

    gate = gate_ref[...]
    routed = gate[:, 0:1] * ybuf[slot, 0]
    for k in range(1, TOP_K):
        routed = routed + gate[:, k:k + 1] * ybuf[slot, k]
    x2 = x1_ref[...] + mod_ref[0, 5:6, :] * (routed + shared)
    if final:
        ms = jnp.mean(x2 * x2, axis=-1, keepdims=True)
        x2 = x2 * lax.rsqrt(ms + NORM_EPS) * fg_ref[...]
    o_ref[...] = x2


def _combine(dest, y, x1, h2, gate, mod, sg, su, sd, final_g, n_lat):
    m, d = x1.shape
    tc = COMBINE_TILE
    n_steps = m // tc
    final = final_g is not None
    dest3 = dest.reshape(n_steps, 1, tc * TOP_K)
    in_specs = [pl.BlockSpec((1, 1, tc * TOP_K), lambda i: (i, 0, 0), memory_space=pltpu.SMEM),
                pl.BlockSpec((1, 1, tc * TOP_K), lambda i: (jnp.minimum(i + 1, n_steps - 1), 0, 0),
                             memory_space=pltpu.SMEM),
                pl.BlockSpec(memory_space=pl.ANY),
                _row_spec(tc, d), _row_spec(tc, d), _row_spec(tc, LANES), _mod_spec(d, n_lat // tc),
                _full_spec(sg.shape), _full_spec(su.shape), _full_spec(sd.shape)]
    args = [dest3, dest3, y, x1, h2, gate, mod, sg, su, sd]
    if final:
        in_specs.append(_full_spec((1, d)))
        args.append(final_g.reshape(1, d))
    return pl.pallas_call(
        functools.partial(_combine_kernel, final=final, n_steps=n_steps),
        grid=(n_steps,),
        in_specs=in_specs,
        out_specs=_row_spec(tc, d),
        out_shape=jax.ShapeDtypeStruct((m, d), F32),
        scratch_shapes=[pltpu.VMEM((2, TOP_K, tc, d), F32), pltpu.SemaphoreType.DMA((2,))],
        name="combine",
        compiler_params=_params("arbitrary", disable_bounds_checks=True),
    )(*args)


def _sorted_rows(m, n_experts, bm):
    return -(-(m * TOP_K + n_experts * (bm - 1)) // bm) * bm


def _routing_tables(eidx, rank, counts, bm):
    padded = (counts + bm - 1) // bm * bm
    pad_start = jnp.cumsum(padded) - padded
    experts = jnp.arange(counts.shape[0], dtype=I32)
    start_of_pick = jnp.sum(jnp.where(eidx[:, :, None] == experts, pad_start, 0), axis=-1)
    return pad_start.astype(I32), (start_of_pick + rank).astype(I32)


def kernel(x, c, ctx, c_ctx, w_mod, b_mod, norm1_g, norm2_g, final_norm_g, a_w_in, a_w_out, a_q_norm,
           a_k_norm, b_conv_w, b_conv_b, b_rgate_w, b_rgate_b, b_igate_w, b_igate_b, b_lambda, c_w_in,
           c_w_out, c_sink, moe_router_w, moe_router_b, moe_w_gate, moe_w_up, moe_w_down,
           moe_shared_gate, moe_shared_up, moe_shared_down):
    batch, n_lat, d = x.shape
    n_ctx = ctx.shape[1]
    depth = w_mod.shape[0]
    n_experts = moe_router_w.shape[2]
    assert batch == 1 and n_ctx == ROW_TILE and n_lat % ROW_TILE == 0
    assert n_lat >= WIN_Q_TILE + 2 * WINDOW and n_experts <= LANES

    tk = min(ATT_KV_CHUNK, n_lat)
    xs = jnp.concatenate([x[0], ctx[0]], axis=0)
    mods = _ada_params(jnp.stack([c[0], c_ctx]), w_mod, b_mod).reshape(depth, 2, N_MOD, d)
    tables_a = _rope_tables(n_lat, n_ctx, A_HEAD_DIM)
    tables_c = _rope_tables(n_lat, n_ctx, C_HEAD_DIM)

    pad_e = LANES - n_experts
    rw = jnp.pad(moe_router_w, ((0, 0), (0, 0), (0, pad_e)))
    rwh = rw.astype(BF16)
    rwl = (rw - rwh.astype(F32)).astype(BF16)
    rb = jnp.pad(moe_router_b, ((0, 0), (0, pad_e)), constant_values=NEG_INF).reshape(depth, 1, LANES)
    sorted_x = jnp.zeros((_sorted_rows(n_lat + n_ctx, n_experts, EXPERT_ROWS), d), F32)
    sorted_y = jnp.zeros_like(sorted_x)

    for layer in range(depth):
        i = layer // 2
        mod = mods[layer]
        if layer % 2 == 0:
            q, k, v, xr, gr = _in_proj_even(xs, norm1_g[layer], mod, a_w_in[i].astype(BF16), tables_a,
                                            a_q_norm[i], a_k_norm[i], n_lat)
            att = _dense_attention(q, k, v, n_lat, tk)
            wcat = jnp.concatenate([b_rgate_w[i], b_igate_w[i]], axis=-1).astype(BF16)
            rec = _rglru(xr, gr, b_conv_w[i], b_conv_b[i], wcat, b_rgate_b[i], b_igate_b[i],
                         b_lambda[i], n_lat)
            w_out = a_w_out[i].astype(BF16)
            a_q = att.shape[1]
            atts, wouts = (att, rec), (w_out[:a_q], w_out[a_q:])
        else:
            q, klo, khi, vlo, vhi = _in_proj_odd(xs, norm1_g[layer], mod, c_w_in[i].astype(BF16),
                                                 tables_c, n_lat)
            att = _window_attention(q, klo, khi, vlo, vhi, c_sink[i], n_lat)
            atts, wouts = (att,), (c_w_out[i].astype(BF16),)
        x1, h2, eidx, rank, gate, cnt = _post_mixer(atts, wouts, xs, mod, norm2_g[layer],
                                                    rwh[layer], rwl[layer], rb[layer], n_lat)
        counts = cnt[0, :n_experts].astype(I32)
        pad_start, dest = _routing_tables(eidx[:, :TOP_K], rank[:, :TOP_K], counts, EXPERT_ROWS)
        sorted_x = _dispatch(dest, h2, sorted_x)
        sorted_y = _experts(sorted_x, sorted_y, pad_start, counts, layer, moe_w_gate, moe_w_up, moe_w_down)
        xs = _combine(dest, sorted_y, x1, h2, gate, mod, moe_shared_gate[layer].astype(BF16),
                      moe_shared_up[layer].astype(BF16), moe_shared_down[layer].astype(BF16),
                      final_norm_g if layer == depth - 1 else None, n_lat)
    return xs[:n_lat].reshape(batch, n_lat, d)
```

```python
import functools

import jax
import jax.numpy as jnp
from jax import lax
from jax.experimental import pallas as pl
from jax.experimental.pallas import tpu as pltpu

F32 = jnp.float32
BF16 = jnp.bfloat16
I32 = jnp.int32

NORM_EPS = 1e-6
ROPE_THETA = 10000.0
GRID_W = 64
N_MOD = 6
A_HEAD_DIM = 128
A_GROUP = 4
B_HEADS = 8
CONV_W = 4
RG_C = 8.0
C_HEAD_DIM = 64
C_GROUP = 8
WINDOW = 128
TOP_K = 8
ROUTED_SCALE = 2.5
NEG_INF = -1e30
LOG2_E = 1.4426950408889634

LANES = 128
SUBLANES = 8
ROW_TILE = 256
ATT_Q_TILE = 256
ATT_KV_CHUNK = 1024
WIN_Q_TILE = 128
EXPERT_ROWS = 256
COMBINE_TILE = 128
ADA_COLS = 512
VMEM_LIMIT = 48 * 1024 * 1024


def _params(*sem, **kw):
    return pltpu.CompilerParams(dimension_semantics=sem, vmem_limit_bytes=VMEM_LIMIT, **kw)


def _modulate(xf, g, shift, scale):
    ms = jnp.mean(xf * xf, axis=-1, keepdims=True)
    y = xf * lax.rsqrt(ms + NORM_EPS) * g
    return y * (1.0 + scale) + shift


def _silu(x):
    return x * jax.nn.sigmoid(x)


def _dot(a, b):
    return jnp.dot(a, b, preferred_element_type=F32)


def _dot_nt(a, b):
    return lax.dot_general(a, b, (((1,), (1,)), ((), ())), preferred_element_type=F32)


def _ada_kernel(cond_ref, w_ref, b_ref, o_ref):
    tn = w_ref.shape[2]
    for cnd in range(2):
        s = _silu(cond_ref[cnd])
        for j in range(tn // LANES):
            cols = slice(j * LANES, (j + 1) * LANES)
            acc = jnp.sum(w_ref[0, :, cols] * s, axis=0, keepdims=True)
            o_ref[0, cnd:cnd + 1, cols] = acc + b_ref[0, :, cols]


def _ada_params(cond, w_mod, b_mod):
    depth, d, n = w_mod.shape
    cond_b = jnp.broadcast_to(cond[:, :, None], (2, d, LANES))
    return pl.pallas_call(
        _ada_kernel,
        grid=(depth, n // ADA_COLS),
        in_specs=[pl.BlockSpec((2, d, LANES), lambda l, j: (0, 0, 0)),
                  pl.BlockSpec((1, d, ADA_COLS), lambda l, j: (l, 0, j)),
                  pl.BlockSpec((1, 1, ADA_COLS), lambda l, j: (l, 0, j))],
        out_specs=pl.BlockSpec((1, 2, ADA_COLS), lambda l, j: (l, 0, j)),
        out_shape=jax.ShapeDtypeStruct((depth, 2, n), F32),
        name="ada_params",
        compiler_params=_params("parallel", "parallel"),
    )(cond_b, w_mod, b_mod.reshape(depth, 1, n))


def _rope_tables(n_lat, n_ctx, head_dim):
    n_rows = n_lat // GRID_W
    rows = jnp.repeat(jnp.arange(n_rows, dtype=F32), GRID_W)
    cols = jnp.tile(jnp.arange(GRID_W, dtype=F32), n_rows)
    d_axis = head_dim // 2
    inv = ROPE_THETA ** (-jnp.arange(0, d_axis, 2, dtype=F32) / d_axis)
    ar = rows[:, None] * inv
    ac = cols[:, None] * inv
    ang = jnp.concatenate([ar, ar, ac, ac], axis=-1)
    ang = jnp.tile(ang, (1, LANES // head_dim))
    chunk = head_dim // 4
    even = (jnp.arange(LANES) // chunk) % 2 == 0
    cos, sin = jnp.cos(ang), jnp.sin(ang)
    sa = jnp.where(even, -sin, 0.0)
    sb = jnp.where(even, 0.0, sin)
    pad = ((0, n_ctx), (0, 0))
    return (jnp.pad(cos, pad, constant_values=1.0), jnp.pad(sa, pad), jnp.pad(sb, pad))


def _rope(y, cos, sa, sb, chunk):
    return y * cos + pltpu.roll(y, LANES - chunk, 1) * sa + pltpu.roll(y, chunk, 1) * sb


def _in_proj_even_kernel(x_ref, g_ref, mod_ref, w_ref, cos_ref, sa_ref, sb_ref, qn_ref, kn_ref,
                         q_ref, k_ref, v_ref, xr_ref, gr_ref):
    h = _modulate(x_ref[...], g_ref[...], mod_ref[0, 0:1, :], mod_ref[0, 1:2, :])
    z = _dot(h.astype(BF16), w_ref[...])
    cos, sa, sb = cos_ref[...], sa_ref[...], sb_ref[...]
    a_q, a_kv, b_w = q_ref.shape[1], k_ref.shape[1], xr_ref.shape[1]

    def norm_rope(zh, gain):
        ms = jnp.mean(zh * zh, axis=-1, keepdims=True)
        return _rope(zh * lax.rsqrt(ms + NORM_EPS) * gain, cos, sa, sb, A_HEAD_DIM // 4)

    scale = A_HEAD_DIM ** -0.5 * LOG2_E
    for hd in range(a_q // LANES):
        cols = slice(hd * LANES, (hd + 1) * LANES)
        q_ref[:, cols] = (norm_rope(z[:, cols], qn_ref[...]) * scale).astype(BF16)
    for hd in range(a_kv // LANES):
        cols = slice(hd * LANES, (hd + 1) * LANES)
        k_ref[:, cols] = norm_rope(z[:, a_q + hd * LANES:a_q + (hd + 1) * LANES], kn_ref[...]).astype(BF16)
    v_ref[...] = z[:, a_q + a_kv:a_q + 2 * a_kv].astype(BF16)
    xr_ref[...] = z[:, a_q + 2 * a_kv:a_q + 2 * a_kv + b_w]
    gr_ref[...] = z[:, a_q + 2 * a_kv + b_w:]


def _row_spec(tm, n):
    return pl.BlockSpec((tm, n), lambda i: (i, 0))


def _full_spec(shape):
    nd = len(shape)
    return pl.BlockSpec(shape, lambda i: (0,) * nd)


def _mod_spec(d, n_lat_tiles):
    return pl.BlockSpec((1, N_MOD, d), lambda i: (jnp.where(i >= n_lat_tiles, 1, 0), 0, 0))


def _in_proj_even(x, g, mod, w, tables, qn, kn, n_lat):
    m, d = x.shape
    tm = ROW_TILE
    a_q = d // 2
    a_kv = a_q // A_GROUP
    b_w = d // 2
    cos, sa, sb = tables
    return pl.pallas_call(
        _in_proj_even_kernel,
        grid=(m // tm,),
        in_specs=[_row_spec(tm, d), _full_spec((1, d)), _mod_spec(d, n_lat // tm), _full_spec(w.shape),
                  _row_spec(tm, LANES), _row_spec(tm, LANES), _row_spec(tm, LANES),
                  _full_spec((1, LANES)), _full_spec((1, LANES))],
        out_specs=[_row_spec(tm, a_q), _row_spec(tm, a_kv), _row_spec(tm, a_kv),
                   _row_spec(tm, b_w), _row_spec(tm, b_w)],
        out_shape=[jax.ShapeDtypeStruct((m, a_q), BF16), jax.ShapeDtypeStruct((m, a_kv), BF16),
                   jax.ShapeDtypeStruct((m, a_kv), BF16), jax.ShapeDtypeStruct((m, b_w), F32),
                   jax.ShapeDtypeStruct((m, b_w), F32)],
        name="in_proj_even",
        compiler_params=_params("parallel"),
    )(x, g.reshape(1, d), mod, w, cos, sa, sb, qn.reshape(1, LANES), kn.reshape(1, LANES))


def _in_proj_odd_kernel(x_ref, g_ref, mod_ref, w_ref, cos_ref, sa_ref, sb_ref,
                        q_ref, qs_ref, k_ref, v_ref):
    h = _modulate(x_ref[...], g_ref[...], mod_ref[0, 0:1, :], mod_ref[0, 1:2, :])
    z = _dot(h.astype(BF16), w_ref[...])
    cos, sa, sb = cos_ref[...], sa_ref[...], sb_ref[...]
    tm = x_ref.shape[0]
    c_q = q_ref.shape[1]
    c_kv = k_ref.shape[1] // 2
    scale = C_HEAD_DIM ** -0.5 * LOG2_E
    for j in range(c_q // LANES):
        cols = slice(j * LANES, (j + 1) * LANES)
        qj = _rope(z[:, cols], cos, sa, sb, C_HEAD_DIM // 4) * scale
        q_ref[:, cols] = qj.astype(BF16)
        qs_ref[:, cols] = pltpu.roll(qj, C_HEAD_DIM, 1).astype(BF16)
    lo = lax.broadcasted_iota(I32, (tm, LANES), 1) < C_HEAD_DIM

    def expand(pair, out_ref, j):
        swapped = pltpu.roll(pair, C_HEAD_DIM, 1)
        out_ref[:, 2 * j * LANES:(2 * j + 1) * LANES] = jnp.where(lo, pair, 0.0).astype(BF16)
        out_ref[:, (2 * j + 1) * LANES:(2 * j + 2) * LANES] = jnp.where(lo, swapped, 0.0).astype(BF16)

    for j in range(c_kv // LANES):
        k0 = c_q + j * LANES
        v0 = c_q + c_kv + j * LANES
        expand(_rope(z[:, k0:k0 + LANES], cos, sa, sb, C_HEAD_DIM // 4), k_ref, j)
        expand(z[:, v0:v0 + LANES], v_ref, j)


def _in_proj_odd(x, g, mod, w, tables, n_lat):
    m, d = x.shape
    tm = ROW_TILE
    c_q = d
    c_kv = d // C_GROUP
    cos, sa, sb = tables
    kv_shape = jax.ShapeDtypeStruct((m, 2 * c_kv), BF16)
    return pl.pallas_call(
        _in_proj_odd_kernel,
        grid=(m // tm,),
        in_specs=[_row_spec(tm, d), _full_spec((1, d)), _mod_spec(d, n_lat // tm), _full_spec(w.shape),
                  _row_spec(tm, LANES), _row_spec(tm, LANES), _row_spec(tm, LANES)],
        out_specs=[_row_spec(tm, c_q)] * 2 + [_row_spec(tm, 2 * c_kv)] * 2,
        out_shape=[jax.ShapeDtypeStruct((m, c_q), BF16)] * 2 + [kv_shape, kv_shape],
        name="in_proj_odd",
        compiler_params=_params("parallel"),
    )(x, g.reshape(1, d), mod, w, cos, sa, sb)


def _dense_attn_kernel(q_ref, k_ref, v_ref, o_ref, m_ref, l_ref, acc_ref, *, n_lat, n_ctx, tk):
    tq = q_ref.shape[0]
    is_lat = pl.program_id(1) < n_lat // tq
    q = jnp.concatenate([q_ref[:, g * LANES:(g + 1) * LANES] for g in range(A_GROUP)], axis=0)
    m_ref[...] = jnp.full(m_ref.shape, NEG_INF, F32)
    l_ref[...] = jnp.zeros(l_ref.shape, F32)
    acc_ref[...] = jnp.zeros(acc_ref.shape, F32)

    def step(kc, vc):
        s = _dot_nt(q, kc)
        m_old = m_ref[...]
        m_new = jnp.maximum(m_old, jnp.max(s, axis=-1, keepdims=True))
        alpha = jnp.exp2(m_old - m_new)
        p = jnp.exp2(s - m_new)
        l_ref[...] = alpha * l_ref[...] + jnp.sum(p, axis=-1, keepdims=True)
        acc_ref[...] = alpha * acc_ref[...] + _dot(p.astype(BF16), vc)
        m_ref[...] = m_new

    @pl.when(is_lat)
    def _():
        def body(i, carry):
            start = pl.multiple_of(i * tk, tk)
            step(k_ref[pl.ds(start, tk), :], v_ref[pl.ds(start, tk), :])
            return carry
        lax.fori_loop(0, n_lat // tk, body, 0, unroll=4)

    step(k_ref[pl.ds(n_lat, n_ctx), :], v_ref[pl.ds(n_lat, n_ctx), :])
    out = acc_ref[...] / l_ref[...]
    for g in range(A_GROUP):
        o_ref[:, g * LANES:(g + 1) * LANES] = out[g * tq:(g + 1) * tq].astype(BF16)


def _dense_attention(q, k, v, n_lat, tk):
    m, a_q = q.shape
    n_kv = k.shape[1] // LANES
    tq = ATT_Q_TILE
    gw = A_GROUP * LANES
    rows = A_GROUP * tq
    return pl.pallas_call(
        functools.partial(_dense_attn_kernel, n_lat=n_lat, n_ctx=m - n_lat, tk=tk),
        grid=(n_kv, m // tq),
        in_specs=[pl.BlockSpec((tq, gw), lambda kh, i: (i, kh)),
                  pl.BlockSpec((m, LANES), lambda kh, i: (0, kh)),
                  pl.BlockSpec((m, LANES), lambda kh, i: (0, kh))],
        out_specs=pl.BlockSpec((tq, gw), lambda kh, i: (i, kh)),
        out_shape=jax.ShapeDtypeStruct((m, a_q), BF16),
        scratch_shapes=[pltpu.VMEM((rows, 1), F32), pltpu.VMEM((rows, 1), F32),
                        pltpu.VMEM((rows, LANES), F32)],
        name="dense_attention",
        compiler_params=_params("parallel", "parallel"),
    )(q, k, v)


def _window_attn_kernel(sink_ref, q_ref, qs_ref, k_ref, v_ref, o_ref, *, n_lat, n_ctx):
    tq = q_ref.shape[0]
    n_pairs = C_GROUP // 2
    n_kv = k_ref.shape[1] // LANES
    span = tq + 2 * WINDOW
    b = pl.program_id(0)
    is_lat = b < n_lat // tq
    ws = pl.multiple_of(jnp.clip((b - 1) * tq, 0, n_lat - span), tq)
    off = jnp.where(is_lat, ws - b * tq, 4 * span)
    rel = (lax.broadcasted_iota(I32, (tq, span), 1) - lax.broadcasted_iota(I32, (tq, span), 0)) + off
    valid = jnp.concatenate([jnp.abs(rel) <= WINDOW] * C_GROUP, axis=0)
    rows = n_pairs * tq
    for g in range(n_kv):
        slot = slice(g * LANES, (g + 1) * LANES)
        pairs = [slice((g * n_pairs + j) * LANES, (g * n_pairs + j + 1) * LANES) for j in range(n_pairs)]
        q2 = jnp.concatenate([q_ref[:, p] for p in pairs] + [qs_ref[:, p] for p in pairs], axis=0)
        sk = jnp.concatenate(
            [jnp.full((tq, 1), sink_ref[g * C_GROUP + 2 * j + hi] * LOG2_E, F32)
             for hi in range(2) for j in range(n_pairs)], axis=0)
        s_c = _dot_nt(q2, k_ref[pl.ds(n_lat, n_ctx), slot])
        s_w = jnp.where(valid, _dot_nt(q2, k_ref[pl.ds(ws, span), slot]), NEG_INF)
        mx = jnp.maximum(jnp.maximum(jnp.max(s_c, axis=-1, keepdims=True),
                                     jnp.max(s_w, axis=-1, keepdims=True)), sk)
        e_c = jnp.exp2(s_c - mx)
        e_w = jnp.exp2(s_w - mx)
        den = (jnp.sum(e_c, axis=-1, keepdims=True) + jnp.sum(e_w, axis=-1, keepdims=True)
               + jnp.exp2(sk - mx))
        o2 = (_dot(e_c.astype(BF16), v_ref[pl.ds(n_lat, n_ctx), slot])
              + _dot(e_w.astype(BF16), v_ref[pl.ds(ws, span), slot])) * (1.0 / den)
        out = o2[:rows] + pltpu.roll(o2[rows:], C_HEAD_DIM, 1)
        for j, p in enumerate(pairs):
            o_ref[:, p] = out[j * tq:(j + 1) * tq].astype(BF16)


def _window_attention(q, qs, k, v, sink, n_lat):
    m, c_q = q.shape
    tq = WIN_Q_TILE
    q_spec = pl.BlockSpec((tq, c_q), lambda b, s: (b, 0))
    kv_spec = pl.BlockSpec(k.shape, lambda b, s: (0, 0), pipeline_mode=pl.Buffered(1))
    return pl.pallas_call(
        functools.partial(_window_attn_kernel, n_lat=n_lat, n_ctx=m - n_lat),
        grid_spec=pltpu.PrefetchScalarGridSpec(
            num_scalar_prefetch=1,
            grid=(m // tq,),
            in_specs=[q_spec, q_spec, kv_spec, kv_spec],
            out_specs=q_spec),
        out_shape=jax.ShapeDtypeStruct((m, c_q), BF16),
        name="window_attention",
        compiler_params=_params("parallel"),
    )(sink, q, qs, k, v)


def _rglru_kernel(*refs, reverse, n_tiles, n_lat_tiles):
    if reverse:
        (xp_ref, x_ref, xn_ref, cw_ref, cb_ref, w_ref, rb_ref, ib_ref, lam_ref, hf_ref, gr_ref,
         out_ref, xe_scr, a_scr, u_scr, h_scr, hb_scr) = refs
    else:
        (xp_ref, x_ref, xn_ref, cw_ref, cb_ref, w_ref, rb_ref, ib_ref, lam_ref,
         out_ref, xe_scr, a_scr, u_scr, h_scr) = refs
    tm, bw = x_ref.shape
    pid = pl.program_id(0)
    tile = (n_tiles - 1 - pid) if reverse else (pid + n_lat_tiles) % n_tiles
    seq_start = (tile == 0) | (tile == n_lat_tiles)
    seq_end = (tile == n_lat_tiles - 1) | (tile == n_tiles - 1)

    @pl.when(pid == 0)
    def _():
        h_scr[...] = jnp.zeros(h_scr.shape, F32)

    xe_scr[0:SUBLANES, :] = jnp.where(seq_start, 0.0, xp_ref[...])
    xe_scr[SUBLANES:SUBLANES + tm, :] = x_ref[...]
    xe_scr[SUBLANES + tm:, :] = jnp.where(seq_end, 0.0, xn_ref[...])
    left = CONV_W // 2
    xc = xe_scr[SUBLANES - left:SUBLANES - left + tm, :] * cw_ref[0:1, :]
    for j in range(1, CONV_W):
        s0 = SUBLANES - left + j
        xc = xc + xe_scr[s0:s0 + tm, :] * cw_ref[j:j + 1, :]
    xc = xc + cb_ref[...]

    blk = bw // B_HEADS
    for hd in range(B_HEADS):
        cols = slice(hd * blk, (hd + 1) * blk)
        xh = xc[:, cols]
        zz = _dot(xh.astype(BF16), w_ref[0, hd])
        r = jax.nn.sigmoid(zz[:, :blk] + rb_ref[0, :, cols])
        gi = jax.nn.sigmoid(zz[:, blk:] + ib_ref[0, :, cols])
        lam = lam_ref[0, :, cols]
        log_sig = -(jnp.maximum(-lam, 0.0) + jnp.log1p(jnp.exp(-jnp.abs(lam))))
        log_a = RG_C * r * log_sig
        th = jnp.tanh(log_a)
        a_scr[:, cols] = jnp.exp(log_a)
        u_scr[:, cols] = jnp.sqrt(-2.0 * th / (1.0 - th)) * (gi * xh)

    dst = hb_scr if reverse else out_ref

    def body(j, h):
        t = (tm - 1 - j) if reverse else j
        h = a_scr[pl.ds(t, 1), :] * h + u_scr[pl.ds(t, 1), :]
        dst[pl.ds(t, 1), :] = h
        return h

    h_scr[...] = lax.fori_loop(0, tm, body, h_scr[...], unroll=8)

    if reverse:
        gr = gr_ref[...]
        cdf = 0.5 * (1.0 + jnp.tanh(0.7978845608028654 * (gr + 0.044715 * (gr * gr * gr))))
        out_ref[...] = ((hf_ref[...] + hb_scr[...]) * (gr * cdf)).astype(BF16)


def _rglru(xr, gr, conv_w, conv_b, wcat, rgate_b, igate_b, lam, n_lat):
    m, bw = xr.shape
    tm = ROW_TILE
    n_tiles, n_lat_tiles = m // tm, n_lat // tm
    per_tile = tm // SUBLANES
    n_sub = m // SUBLANES
    blk = bw // B_HEADS

    def run(reverse, extra_in):
        d = 1 if reverse else 0
        if reverse:
            tile = lambda i: n_tiles - 1 - i
        else:
            tile = lambda i: (i + n_lat_tiles) % n_tiles
        row = pl.BlockSpec((tm, bw), lambda i: (tile(i), 0))
        in_specs = [pl.BlockSpec((SUBLANES, bw), lambda i: (jnp.maximum(tile(i) * per_tile - 1, 0), 0)),
                    row,
                    pl.BlockSpec((SUBLANES, bw), lambda i: (jnp.minimum((tile(i) + 1) * per_tile, n_sub - 1), 0)),
                    _full_spec((CONV_W, bw)), _full_spec((1, bw)),
                    pl.BlockSpec((1, B_HEADS, blk, 2 * blk), lambda i: (d, 0, 0, 0)),
                    pl.BlockSpec((1, 1, bw), lambda i: (d, 0, 0)),
                    pl.BlockSpec((1, 1, bw), lambda i: (d, 0, 0)),
                    pl.BlockSpec((1, 1, bw), lambda i: (d, 0, 0))] + [row] * len(extra_in)
        scratch = [pltpu.VMEM((tm + 2 * SUBLANES, bw), F32), pltpu.VMEM((tm, bw), F32),
                   pltpu.VMEM((tm, bw), F32), pltpu.VMEM((1, bw), F32)]
        if reverse:
            scratch.append(pltpu.VMEM((tm, bw), F32))
        return pl.pallas_call(
            functools.partial(_rglru_kernel, reverse=reverse, n_tiles=n_tiles, n_lat_tiles=n_lat_tiles),
            grid=(n_tiles,),
            in_specs=in_specs,
            out_specs=row,
            out_shape=jax.ShapeDtypeStruct((m, bw), BF16 if reverse else F32),
            scratch_shapes=scratch,
            name="rglru_reverse" if reverse else "rglru_forward",
            compiler_params=_params("arbitrary"),
        )(xr, xr, xr, conv_w, conv_b.reshape(1, bw), wcat, rgate_b.reshape(2, 1, bw),
          igate_b.reshape(2, 1, bw), lam.reshape(2, 1, bw), *extra_in)

    hf = run(False, ())
    return run(True, (hf, gr))


def _post_mixer_kernel(*refs, n_att):
    att_refs = refs[:n_att]
    wout_refs = refs[n_att:2 * n_att]
    (x_ref, mod_ref, g2_ref, rwh_ref, rwl_ref, rb_ref,
     x1_ref, h2_ref, eidx_ref, rank_ref, gate_ref, cnt_ref, cnt_scr) = refs[2 * n_att:]
    tm = x_ref.shape[0]

    @pl.when(pl.program_id(0) == 0)
    def _():
        cnt_scr[...] = jnp.zeros(cnt_scr.shape, F32)

    o = _dot(att_refs[0][...], wout_refs[0][...])
    for a_ref, w_ref in zip(att_refs[1:], wout_refs[1:]):
        o = o + _dot(a_ref[...], w_ref[...])
    x1 = x_ref[...] + mod_ref[0, 2:3, :] * o
    x1_ref[...] = x1
    h2 = _modulate(x1, g2_ref[...], mod_ref[0, 3:4, :], mod_ref[0, 4:5, :])
    h2_ref[...] = h2

    hh = h2.astype(BF16)
    hl = (h2 - hh.astype(F32)).astype(BF16)
    logits = _dot(hh, rwh_ref[...]) + _dot(hl, rwh_ref[...]) + _dot(hh, rwl_ref[...])
    scores = jax.nn.sigmoid(logits)
    sel = scores + rb_ref[...]
    lane = lax.broadcasted_iota(I32, (tm, LANES), 1)
    picked = jnp.zeros((tm, LANES), jnp.bool_)
    idxs, vals = [], []
    for _ in range(TOP_K):
        mx = jnp.max(sel, axis=-1, keepdims=True)
        idx = jnp.min(jnp.where(sel == mx, lane, LANES), axis=-1, keepdims=True)
        hit = lane == idx
        vals.append(jnp.sum(jnp.where(hit, scores, 0.0), axis=-1, keepdims=True))
        idxs.append(idx)
        sel = jnp.where(hit, -3e38, sel)
        picked = picked | hit
    total = vals[0]
    for v in vals[1:]:
        total = total + v

    pick_f = picked.astype(F32)
    lower = (lax.broadcasted_iota(I32, (tm, tm), 0) > lax.broadcasted_iota(I32, (tm, tm), 1)).astype(BF16)
    rank_dense = _dot(lower, pick_f.astype(BF16)) + cnt_scr[...]
    cnt_scr[...] = cnt_scr[...] + jnp.sum(pick_f, axis=0, keepdims=True)
    cnt_ref[...] = cnt_scr[...]

    eidx = jnp.zeros((tm, LANES), I32)
    rank = jnp.zeros((tm, LANES), I32)
    gate = jnp.zeros((tm, LANES), F32)
    for k in range(TOP_K):
        rk = jnp.sum(jnp.where(lane == idxs[k], rank_dense, 0.0), axis=-1, keepdims=True)
        eidx = jnp.where(lane == k, idxs[k], eidx)
        rank = jnp.where(lane == k, rk.astype(I32), rank)
        gate = jnp.where(lane == k, ROUTED_SCALE * vals[k] / total, gate)
    eidx_ref[...] = eidx
    rank_ref[...] = rank
    gate_ref[...] = gate


def _post_mixer(atts, wouts, x, mod, g2, rwh, rwl, rb, n_lat):
    m, d = x.shape
    tm = ROW_TILE
    n_att = len(atts)
    in_specs = ([_row_spec(tm, a.shape[1]) for a in atts] + [_full_spec(w.shape) for w in wouts]
                + [_row_spec(tm, d), _mod_spec(d, n_lat // tm), _full_spec((1, d)),
                   _full_spec(rwh.shape), _full_spec(rwl.shape), _full_spec((1, LANES))])
    return pl.pallas_call(
        functools.partial(_post_mixer_kernel, n_att=n_att),
        grid=(m // tm,),
        in_specs=in_specs,
        out_specs=[_row_spec(tm, d), _row_spec(tm, d), _row_spec(tm, LANES), _row_spec(tm, LANES),
                   _row_spec(tm, LANES), _full_spec((1, LANES))],
        out_shape=[jax.ShapeDtypeStruct((m, d), F32), jax.ShapeDtypeStruct((m, d), F32),
                   jax.ShapeDtypeStruct((m, LANES), I32), jax.ShapeDtypeStruct((m, LANES), I32),
                   jax.ShapeDtypeStruct((m, LANES), F32), jax.ShapeDtypeStruct((1, LANES), F32)],
        scratch_shapes=[pltpu.VMEM((1, LANES), F32)],
        name="post_mixer",
        compiler_params=_params("arbitrary"),
    )(*atts, *wouts, x, mod, g2.reshape(1, d), rwh, rwl, rb)


def _dispatch_kernel(dest_ref, h2_ref, xs_in, xs_out, sem):
    del xs_in
    tm = h2_ref.shape[0]

    def issue(r, c):
        for k in range(TOP_K):
            pltpu.make_async_copy(h2_ref.at[pl.ds(r, 1)],
                                  xs_out.at[pl.ds(dest_ref[0, 0, r * TOP_K + k], 1)], sem).start()
        return c
    lax.fori_loop(0, tm, issue, 0)
    for _ in range(TOP_K):
        pltpu.make_async_copy(h2_ref, xs_out.at[pl.ds(0, tm)], sem).wait()


def _dispatch(dest, h2, xs):
    m, d = h2.shape
    tm = ROW_TILE
    return pl.pallas_call(
        _dispatch_kernel,
        grid=(m // tm,),
        in_specs=[pl.BlockSpec((1, 1, tm * TOP_K), lambda i: (i, 0, 0), memory_space=pltpu.SMEM),
                  _row_spec(tm, d), pl.BlockSpec(memory_space=pl.ANY)],
        out_specs=pl.BlockSpec(memory_space=pl.ANY),
        out_shape=jax.ShapeDtypeStruct(xs.shape, xs.dtype),
        scratch_shapes=[pltpu.SemaphoreType.DMA],
        input_output_aliases={2: 0},
        name="dispatch",
        compiler_params=_params("arbitrary", disable_bounds_checks=True),
    )(dest.reshape(m // tm, 1, tm * TOP_K), h2, xs)


def _expert_kernel(ps_ref, cnt_ref, xs_hbm, wg_ref, wu_ref, wd_ref, y_hbm,
                   xbuf, ybuf, wgb, wub, wdb, in_sem, out_sem):
    del xs_hbm
    bm = xbuf.shape[1]
    e = pl.program_id(0)
    base = ps_ref[e]
    nblk = (cnt_ref[e] + bm - 1) // bm

    def rows(j):
        return pl.ds(pl.multiple_of(base + j * bm, bm), bm)

    def in_copy(j, slot):
        return pltpu.make_async_copy(y_hbm.at[rows(j)], xbuf.at[slot], in_sem.at[slot])

    def out_copy(j, slot):
        return pltpu.make_async_copy(ybuf.at[slot], y_hbm.at[rows(j)], out_sem.at[slot])

    @pl.when(nblk > 0)
    def _():
        in_copy(0, 0).start()

    wgb[...] = wg_ref[0, 0].astype(BF16)
    wub[...] = wu_ref[0, 0].astype(BF16)
    wdb[...] = wd_ref[0, 0].astype(BF16)

    def block(j, slot):
        in_copy(j, slot).wait()

        @pl.when(j + 1 < nblk)
        def _():
            in_copy(j + 1, 1 - slot).start()

        @pl.when(j >= 2)
        def _():
            out_copy(j - 2, slot).wait()

        xb = xbuf[slot].astype(BF16)
        hm = _silu(_dot(xb, wgb[...])) * _dot(xb, wub[...])
        ybuf[slot] = _dot(hm.astype(BF16), wdb[...])
        out_copy(j, slot).start()

    def pair(jj, c):
        block(2 * jj, 0)

        @pl.when(2 * jj + 1 < nblk)
        def _():
            block(2 * jj + 1, 1)
        return c
    lax.fori_loop(0, (nblk + 1) // 2, pair, 0)

    for back in (1, 2):
        @pl.when(nblk >= back)
        def _():
            out_copy(nblk - back, (nblk - back) % 2).wait()


def _experts(xs, pad_start, counts, layer, w_gate, w_up, w_down):
    _, n_e, d, d_e = w_gate.shape
    bm = EXPERT_ROWS
    return pl.pallas_call(
        _expert_kernel,
        grid_spec=pltpu.PrefetchScalarGridSpec(
            num_scalar_prefetch=2,
            grid=(n_e,),
            in_specs=[pl.BlockSpec(memory_space=pl.ANY),
                      pl.BlockSpec((1, 1, d, d_e), lambda e, ps, cn: (layer, e, 0, 0)),
                      pl.BlockSpec((1, 1, d, d_e), lambda e, ps, cn: (layer, e, 0, 0)),
                      pl.BlockSpec((1, 1, d_e, d), lambda e, ps, cn: (layer, e, 0, 0))],
            out_specs=pl.BlockSpec(memory_space=pl.ANY),
            scratch_shapes=[pltpu.VMEM((2, bm, d), F32), pltpu.VMEM((2, bm, d), F32),
                            pltpu.VMEM((d, d_e), BF16), pltpu.VMEM((d, d_e), BF16),
                            pltpu.VMEM((d_e, d), BF16),
                            pltpu.SemaphoreType.DMA((2,)), pltpu.SemaphoreType.DMA((2,))]),
        out_shape=jax.ShapeDtypeStruct(xs.shape, xs.dtype),
        input_output_aliases={2: 0},
        name="experts",
        compiler_params=_params("arbitrary"),
    )(pad_start, counts, xs, w_gate, w_up, w_down)


def _combine_kernel(*refs, final, n_steps):
    if final:
        (dest_ref, next_dest_ref, y_hbm, x1_ref, h2_ref, gate_ref, mod_ref, sg_ref, su_ref, sd_ref,
         fg_ref, o_ref, ybuf, sem) = refs
    else:
        (dest_ref, next_dest_ref, y_hbm, x1_ref, h2_ref, gate_ref, mod_ref, sg_ref, su_ref, sd_ref,
         o_ref, ybuf, sem) = refs
    tc = x1_ref.shape[0]
    i = pl.program_id(0)
    slot = i % 2

    def gather(d_ref, s):
        def issue(r, c):
            for k in range(TOP_K):
                pltpu.make_async_copy(y_hbm.at[pl.ds(d_ref[0, 0, r * TOP_K + k], 1)],
                                      ybuf.at[s, k, pl.ds(r, 1)], sem.at[s]).start()
            return c
        lax.fori_loop(0, tc, issue, 0)

    @pl.when(i == 0)
    def _():
        gather(dest_ref, 0)

    for s in range(2):
        @pl.when((i + 1 < n_steps) & (slot != s))
        def _():
            gather(next_dest_ref, s)

    hb = h2_ref[...].astype(BF16)
    shared = _dot((_silu(_dot(hb, sg_ref[...])) * _dot(hb, su_ref[...])).astype(BF16), sd_ref[...])

    for k in range(TOP_K):
        pltpu.make_async_copy(y_hbm.at[pl.ds(0, tc)], ybuf.at[slot, k], sem.at[slot]).wait()

    gate = gate_ref[...]
    routed = gate[:, 0:1] * ybuf[slot, 0]
    for k in range(1, TOP_K):
        routed = routed + gate[:, k:k + 1] * ybuf[slot, k]
    x2 = x1_ref[...] + mod_ref[0, 5:6, :] * (routed + shared)
    if final:
        ms = jnp.mean(x2 * x2, axis=-1, keepdims=True)
        x2 = x2 * lax.rsqrt(ms + NORM_EPS) * fg_ref[...]
    o_ref[...] = x2


def _combine(dest, y, x1, h2, gate, mod, sg, su, sd, final_g, n_lat):
    m, d = x1.shape
    tc = COMBINE_TILE
    n_steps = m // tc
    final = final_g is not None
    dest3 = dest.reshape(n_steps, 1, tc * TOP_K)
    in_specs = [pl.BlockSpec((1, 1, tc * TOP_K), lambda i: (i, 0, 0), memory_space=pltpu.SMEM),
                pl.BlockSpec((1, 1, tc * TOP_K), lambda i: (jnp.minimum(i + 1, n_steps - 1), 0, 0),
                             memory_space=pltpu.SMEM),
                pl.BlockSpec(memory_space=pl.ANY),
                _row_spec(tc, d), _row_spec(tc, d), _row_spec(tc, LANES), _mod_spec(d, n_lat // tc),
                _full_spec(sg.shape), _full_spec(su.shape), _full_spec(sd.shape)]
    args = [dest3, dest3, y, x1, h2, gate, mod, sg, su, sd]
    if final:
        in_specs.append(_full_spec((1, d)))
        args.append(final_g.reshape(1, d))
    return pl.pallas_call(
        functools.partial(_combine_kernel, final=final, n_steps=n_steps),
        grid=(n_steps,),
        in_specs=in_specs,
        out_specs=_row_spec(tc, d),
        out_shape=jax.ShapeDtypeStruct((m, d), F32),
        scratch_shapes=[pltpu.VMEM((2, TOP_K, tc, d), F32), pltpu.SemaphoreType.DMA((2,))],
        name="combine",
        compiler_params=_params("arbitrary", disable_bounds_checks=True),
    )(*args)


def _sorted_rows(m, n_experts, bm):
    return -(-(m * TOP_K + n_experts * (bm - 1)) // bm) * bm


def _routing_tables(eidx, rank, counts, bm):
    padded = (counts + bm - 1) // bm * bm
    pad_start = jnp.cumsum(padded) - padded
    experts = jnp.arange(counts.shape[0], dtype=I32)
    start_of_pick = jnp.sum(jnp.where(eidx[:, :, None] == experts, pad_start, 0), axis=-1)
    return pad_start.astype(I32), (start_of_pick + rank).astype(I32)


def kernel(x, c, ctx, c_ctx, w_mod, b_mod, norm1_g, norm2_g, final_norm_g, a_w_in, a_w_out, a_q_norm,
           a_k_norm, b_conv_w, b_conv_b, b_rgate_w, b_rgate_b, b_igate_w, b_igate_b, b_lambda, c_w_in,
           c_w_out, c_sink, moe_router_w, moe_router_b, moe_w_gate, moe_w_up, moe_w_down,
           moe_shared_gate, moe_shared_up, moe_shared_down):
    batch, n_lat, d = x.shape
    n_ctx = ctx.shape[1]
    depth = w_mod.shape[0]
    n_experts = moe_router_w.shape[2]
    assert batch == 1 and n_ctx == ROW_TILE and n_lat % ROW_TILE == 0
    assert n_lat >= WIN_Q_TILE + 2 * WINDOW and n_experts <= LANES

    tk = min(ATT_KV_CHUNK, n_lat)
    xs = jnp.concatenate([x[0], ctx[0]], axis=0)
    mods = _ada_params(jnp.stack([c[0], c_ctx]), w_mod, b_mod).reshape(depth, 2, N_MOD, d)
    tables_a = _rope_tables(n_lat, n_ctx, A_HEAD_DIM)
    tables_c = _rope_tables(n_lat, n_ctx, C_HEAD_DIM)

    pad_e = LANES - n_experts
    rw = jnp.pad(moe_router_w, ((0, 0), (0, 0), (0, pad_e)))
    rwh = rw.astype(BF16)
    rwl = (rw - rwh.astype(F32)).astype(BF16)
    rb = jnp.pad(moe_router_b, ((0, 0), (0, pad_e)), constant_values=NEG_INF).reshape(depth, 1, LANES)
    sorted_x = jnp.zeros((_sorted_rows(n_lat + n_ctx, n_experts, EXPERT_ROWS), d), F32)

    for layer in range(depth):
        i = layer // 2
        mod = mods[layer]
        if layer % 2 == 0:
            q, k, v, xr, gr = _in_proj_even(xs, norm1_g[layer], mod, a_w_in[i].astype(BF16), tables_a,
                                            a_q_norm[i], a_k_norm[i], n_lat)
            att = _dense_attention(q, k, v, n_lat, tk)
            wcat = jnp.concatenate([b_rgate_w[i], b_igate_w[i]], axis=-1).astype(BF16)
            rec = _rglru(xr, gr, b_conv_w[i], b_conv_b[i], wcat, b_rgate_b[i], b_igate_b[i],
                         b_lambda[i], n_lat)
            w_out = a_w_out[i].astype(BF16)
            a_q = att.shape[1]
            atts, wouts = (att, rec), (w_out[:a_q], w_out[a_q:])
        else:
            q, qs, k, v = _in_proj_odd(xs, norm1_g[layer], mod, c_w_in[i].astype(BF16), tables_c, n_lat)
            att = _window_attention(q, qs, k, v, c_sink[i], n_lat)
            atts, wouts = (att,), (c_w_out[i].astype(BF16),)
        x1, h2, eidx, rank, gate, cnt = _post_mixer(atts, wouts, xs, mod, norm2_g[layer],
                                                    rwh[layer], rwl[layer], rb[layer], n_lat)
        counts = cnt[0, :n_experts].astype(I32)
        pad_start, dest = _routing_tables(eidx[:, :TOP_K], rank[:, :TOP_K], counts, EXPERT_ROWS)
        sorted_x = _dispatch(dest, h2, sorted_x)
        sorted_x = _experts(sorted_x, pad_start, counts, layer, moe_w_gate, moe_w_up, moe_w_down)
        xs = _combine(dest, sorted_x, x1, h2, gate, mod, moe_shared_gate[layer].astype(BF16),
                      moe_shared_up[layer].astype(BF16), moe_shared_down[layer].astype(BF16),
                      final_norm_g if layer == depth - 1 else None, n_lat)
    return xs[:n_lat].reshape(batch, n_lat, d)
```

```python
import functools

import jax
import jax.numpy as jnp
from jax import lax
from jax.experimental import pallas as pl
from jax.experimental.pallas import tpu as pltpu

F32 = jnp.float32
BF16 = jnp.bfloat16
I32 = jnp.int32

NORM_EPS = 1e-6
ROPE_THETA = 10000.0
GRID_W = 64
N_MOD = 6
A_HEAD_DIM = 128
A_GROUP = 4
B_HEADS = 8
CONV_W = 4
RG_C = 8.0
C_HEAD_DIM = 64
C_GROUP = 8
WINDOW = 128
TOP_K = 8
ROUTED_SCALE = 2.5
NEG_INF = -1e30
LOG2_E = 1.4426950408889634

LANES = 128
SUBLANES = 8
ROW_TILE = 256
ATT_Q_TILE = 256
ATT_KV_CHUNK = 2048
WIN_Q_TILE = 128
EXPERT_ROWS = 256
COMBINE_TILE = 128
ADA_COLS = 512
VMEM_LIMIT = 48 * 1024 * 1024


def _params(*sem, **kw):
    return pltpu.CompilerParams(dimension_semantics=sem, vmem_limit_bytes=VMEM_LIMIT, **kw)


def _modulate(xf, g, shift, scale):
    ms = jnp.mean(xf * xf, axis=-1, keepdims=True)
    y = xf * lax.rsqrt(ms + NORM_EPS) * g
    return y * (1.0 + scale) + shift


def _silu(x):
    return x * jax.nn.sigmoid(x)


def _dot(a, b):
    return jnp.dot(a, b, preferred_element_type=F32)


def _dot_nt(a, b):
    return lax.dot_general(a, b, (((1,), (1,)), ((), ())), preferred_element_type=F32)


def _ada_kernel(cond_ref, w_ref, b_ref, o_ref):
    tn = w_ref.shape[2]
    for cnd in range(2):
        s = _silu(cond_ref[cnd])
        for j in range(tn // LANES):
            cols = slice(j * LANES, (j + 1) * LANES)
            acc = jnp.sum(w_ref[0, :, cols] * s, axis=0, keepdims=True)
            o_ref[0, cnd:cnd + 1, cols] = acc + b_ref[0, :, cols]


def _ada_params(cond, w_mod, b_mod):
    depth, d, n = w_mod.shape
    cond_b = jnp.broadcast_to(cond[:, :, None], (2, d, LANES))
    return pl.pallas_call(
        _ada_kernel,
        grid=(depth, n // ADA_COLS),
        in_specs=[pl.BlockSpec((2, d, LANES), lambda l, j: (0, 0, 0)),
                  pl.BlockSpec((1, d, ADA_COLS), lambda l, j: (l, 0, j)),
                  pl.BlockSpec((1, 1, ADA_COLS), lambda l, j: (l, 0, j))],
        out_specs=pl.BlockSpec((1, 2, ADA_COLS), lambda l, j: (l, 0, j)),
        out_shape=jax.ShapeDtypeStruct((depth, 2, n), F32),
        name="ada_params",
        compiler_params=_params("parallel", "parallel"),
    )(cond_b, w_mod, b_mod.reshape(depth, 1, n))


def _rope_tables(n_lat, n_ctx, head_dim):
    n_rows = n_lat // GRID_W
    rows = jnp.repeat(jnp.arange(n_rows, dtype=F32), GRID_W)
    cols = jnp.tile(jnp.arange(GRID_W, dtype=F32), n_rows)
    d_axis = head_dim // 2
    inv = ROPE_THETA ** (-jnp.arange(0, d_axis, 2, dtype=F32) / d_axis)
    ar = rows[:, None] * inv
    ac = cols[:, None] * inv
    ang = jnp.concatenate([ar, ar, ac, ac], axis=-1)
    ang = jnp.tile(ang, (1, LANES // head_dim))
    chunk = head_dim // 4
    even = (jnp.arange(LANES) // chunk) % 2 == 0
    cos, sin = jnp.cos(ang), jnp.sin(ang)
    sa = jnp.where(even, -sin, 0.0)
    sb = jnp.where(even, 0.0, sin)
    pad = ((0, n_ctx), (0, 0))
    return (jnp.pad(cos, pad, constant_values=1.0), jnp.pad(sa, pad), jnp.pad(sb, pad))


def _rope(y, cos, sa, sb, chunk):
    return y * cos + pltpu.roll(y, LANES - chunk, 1) * sa + pltpu.roll(y, chunk, 1) * sb


def _in_proj_even_kernel(x_ref, g_ref, mod_ref, w_ref, cos_ref, sa_ref, sb_ref, qn_ref, kn_ref,
                         q_ref, k_ref, v_ref, xr_ref, gr_ref):
    h = _modulate(x_ref[...], g_ref[...], mod_ref[0, 0:1, :], mod_ref[0, 1:2, :])
    z = _dot(h.astype(BF16), w_ref[...])
    cos, sa, sb = cos_ref[...], sa_ref[...], sb_ref[...]
    a_q, a_kv, b_w = q_ref.shape[1], k_ref.shape[1], xr_ref.shape[1]

    def norm_rope(zh, gain):
        ms = jnp.mean(zh * zh, axis=-1, keepdims=True)
        return _rope(zh * lax.rsqrt(ms + NORM_EPS) * gain, cos, sa, sb, A_HEAD_DIM // 4)

    scale = A_HEAD_DIM ** -0.5 * LOG2_E
    for hd in range(a_q // LANES):
        cols = slice(hd * LANES, (hd + 1) * LANES)
        q_ref[:, cols] = (norm_rope(z[:, cols], qn_ref[...]) * scale).astype(BF16)
    for hd in range(a_kv // LANES):
        cols = slice(hd * LANES, (hd + 1) * LANES)
        k_ref[:, cols] = norm_rope(z[:, a_q + hd * LANES:a_q + (hd + 1) * LANES], kn_ref[...]).astype(BF16)
    v_ref[...] = z[:, a_q + a_kv:a_q + 2 * a_kv].astype(BF16)
    xr_ref[...] = z[:, a_q + 2 * a_kv:a_q + 2 * a_kv + b_w]
    gr_ref[...] = z[:, a_q + 2 * a_kv + b_w:]


def _row_spec(tm, n):
    return pl.BlockSpec((tm, n), lambda i: (i, 0))


def _full_spec(shape):
    nd = len(shape)
    return pl.BlockSpec(shape, lambda i: (0,) * nd)


def _mod_spec(d, n_lat_tiles):
    return pl.BlockSpec((1, N_MOD, d), lambda i: (jnp.where(i >= n_lat_tiles, 1, 0), 0, 0))


def _in_proj_even(x, g, mod, w, tables, qn, kn, n_lat):
    m, d = x.shape
    tm = ROW_TILE
    a_q = d // 2
    a_kv = a_q // A_GROUP
    b_w = d // 2
    cos, sa, sb = tables
    return pl.pallas_call(
        _in_proj_even_kernel,
        grid=(m // tm,),
        in_specs=[_row_spec(tm, d), _full_spec((1, d)), _mod_spec(d, n_lat // tm), _full_spec(w.shape),
                  _row_spec(tm, LANES), _row_spec(tm, LANES), _row_spec(tm, LANES),
                  _full_spec((1, LANES)), _full_spec((1, LANES))],
        out_specs=[_row_spec(tm, a_q), _row_spec(tm, a_kv), _row_spec(tm, a_kv),
                   _row_spec(tm, b_w), _row_spec(tm, b_w)],
        out_shape=[jax.ShapeDtypeStruct((m, a_q), BF16), jax.ShapeDtypeStruct((m, a_kv), BF16),
                   jax.ShapeDtypeStruct((m, a_kv), BF16), jax.ShapeDtypeStruct((m, b_w), F32),
                   jax.ShapeDtypeStruct((m, b_w), F32)],
        name="in_proj_even",
        compiler_params=_params("parallel"),
    )(x, g.reshape(1, d), mod, w, cos, sa, sb, qn.reshape(1, LANES), kn.reshape(1, LANES))


def _in_proj_odd_kernel(x_ref, g_ref, mod_ref, w_ref, cos_ref, sa_ref, sb_ref,
                        q_ref, qs_ref, k_ref, v_ref):
    h = _modulate(x_ref[...], g_ref[...], mod_ref[0, 0:1, :], mod_ref[0, 1:2, :])
    z = _dot(h.astype(BF16), w_ref[...])
    cos, sa, sb = cos_ref[...], sa_ref[...], sb_ref[...]
    tm = x_ref.shape[0]
    c_q = q_ref.shape[1]
    c_kv = k_ref.shape[1] // 2
    scale = C_HEAD_DIM ** -0.5 * LOG2_E
    for j in range(c_q // LANES):
        cols = slice(j * LANES, (j + 1) * LANES)
        qj = _rope(z[:, cols], cos, sa, sb, C_HEAD_DIM // 4) * scale
        q_ref[:, cols] = qj.astype(BF16)
        qs_ref[:, cols] = pltpu.roll(qj, C_HEAD_DIM, 1).astype(BF16)
    lo = lax.broadcasted_iota(I32, (tm, LANES), 1) < C_HEAD_DIM

    def expand(pair, out_ref, j):
        swapped = pltpu.roll(pair, C_HEAD_DIM, 1)
        out_ref[:, 2 * j * LANES:(2 * j + 1) * LANES] = jnp.where(lo, pair, 0.0).astype(BF16)
        out_ref[:, (2 * j + 1) * LANES:(2 * j + 2) * LANES] = jnp.where(lo, swapped, 0.0).astype(BF16)

    for j in range(c_kv // LANES):
        k0 = c_q + j * LANES
        v0 = c_q + c_kv + j * LANES
        expand(_rope(z[:, k0:k0 + LANES], cos, sa, sb, C_HEAD_DIM // 4), k_ref, j)
        expand(z[:, v0:v0 + LANES], v_ref, j)


def _in_proj_odd(x, g, mod, w, tables, n_lat):
    m, d = x.shape
    tm = ROW_TILE
    c_q = d
    c_kv = d // C_GROUP
    cos, sa, sb = tables
    kv_shape = jax.ShapeDtypeStruct((m, 2 * c_kv), BF16)
    return pl.pallas_call(
        _in_proj_odd_kernel,
        grid=(m // tm,),
        in_specs=[_row_spec(tm, d), _full_spec((1, d)), _mod_spec(d, n_lat // tm), _full_spec(w.shape),
                  _row_spec(tm, LANES), _row_spec(tm, LANES), _row_spec(tm, LANES)],
        out_specs=[_row_spec(tm, c_q)] * 2 + [_row_spec(tm, 2 * c_kv)] * 2,
        out_shape=[jax.ShapeDtypeStruct((m, c_q), BF16)] * 2 + [kv_shape, kv_shape],
        name="in_proj_odd",
        compiler_params=_params("parallel"),
    )(x, g.reshape(1, d), mod, w, cos, sa, sb)


def _dense_attn_kernel(q_ref, k_ref, v_ref, o_ref, m_ref, l_ref, acc_ref, *, n_lat, n_ctx, tk):
    tq = q_ref.shape[0]
    is_lat = pl.program_id(1) < n_lat // tq
    q = jnp.concatenate([q_ref[:, g * LANES:(g + 1) * LANES] for g in range(A_GROUP)], axis=0)
    m_ref[...] = jnp.full(m_ref.shape, NEG_INF, F32)
    l_ref[...] = jnp.zeros(l_ref.shape, F32)
    acc_ref[...] = jnp.zeros(acc_ref.shape, F32)

    def step(kc, vc):
        s = _dot_nt(q, kc)
        m_old = m_ref[...]
        m_new = jnp.maximum(m_old, jnp.max(s, axis=-1, keepdims=True))
        alpha = jnp.exp2(m_old - m_new)
        p = jnp.exp2(s - m_new)
        l_ref[...] = alpha * l_ref[...] + jnp.sum(p, axis=-1, keepdims=True)
        acc_ref[...] = alpha * acc_ref[...] + _dot(p.astype(BF16), vc)
        m_ref[...] = m_new

    @pl.when(is_lat)
    def _():
        def body(i, carry):
            start = pl.multiple_of(i * tk, tk)
            step(k_ref[pl.ds(start, tk), :], v_ref[pl.ds(start, tk), :])
            return carry
        lax.fori_loop(0, n_lat // tk, body, 0, unroll=4)

    step(k_ref[pl.ds(n_lat, n_ctx), :], v_ref[pl.ds(n_lat, n_ctx), :])
    out = acc_ref[...] / l_ref[...]
    for g in range(A_GROUP):
        o_ref[:, g * LANES:(g + 1) * LANES] = out[g * tq:(g + 1) * tq].astype(BF16)


def _dense_attention(q, k, v, n_lat, tk):
    m, a_q = q.shape
    n_kv = k.shape[1] // LANES
    tq = ATT_Q_TILE
    gw = A_GROUP * LANES
    rows = A_GROUP * tq
    return pl.pallas_call(
        functools.partial(_dense_attn_kernel, n_lat=n_lat, n_ctx=m - n_lat, tk=tk),
        grid=(n_kv, m // tq),
        in_specs=[pl.BlockSpec((tq, gw), lambda kh, i: (i, kh)),
                  pl.BlockSpec((m, LANES), lambda kh, i: (0, kh)),
                  pl.BlockSpec((m, LANES), lambda kh, i: (0, kh))],
        out_specs=pl.BlockSpec((tq, gw), lambda kh, i: (i, kh)),
        out_shape=jax.ShapeDtypeStruct((m, a_q), BF16),
        scratch_shapes=[pltpu.VMEM((rows, 1), F32), pltpu.VMEM((rows, 1), F32),
                        pltpu.VMEM((rows, LANES), F32)],
        name="dense_attention",
        compiler_params=_params("parallel", "parallel"),
    )(q, k, v)


def _window_attn_kernel(sink_ref, q_ref, qs_ref, k_ref, v_ref, o_ref, *, n_lat, n_ctx):
    tq = q_ref.shape[0]
    n_pairs = C_GROUP // 2
    n_kv = k_ref.shape[1] // LANES
    span = tq + 2 * WINDOW
    b = pl.program_id(0)
    is_lat = b < n_lat // tq
    ws = pl.multiple_of(jnp.clip((b - 1) * tq, 0, n_lat - span), tq)
    off = jnp.where(is_lat, ws - b * tq, 4 * span)
    rel = (lax.broadcasted_iota(I32, (tq, span), 1) - lax.broadcasted_iota(I32, (tq, span), 0)) + off
    valid = jnp.concatenate([jnp.abs(rel) <= WINDOW] * C_GROUP, axis=0)
    rows = n_pairs * tq
    for g in range(n_kv):
        slot = slice(g * LANES, (g + 1) * LANES)
        pairs = [slice((g * n_pairs + j) * LANES, (g * n_pairs + j + 1) * LANES) for j in range(n_pairs)]
        q2 = jnp.concatenate([q_ref[:, p] for p in pairs] + [qs_ref[:, p] for p in pairs], axis=0)
        sk = jnp.concatenate(
            [jnp.full((tq, 1), sink_ref[g * C_GROUP + 2 * j + hi] * LOG2_E, F32)
             for hi in range(2) for j in range(n_pairs)], axis=0)
        s_c = _dot_nt(q2, k_ref[pl.ds(n_lat, n_ctx), slot])
        s_w = jnp.where(valid, _dot_nt(q2, k_ref[pl.ds(ws, span), slot]), NEG_INF)
        mx = jnp.maximum(jnp.maximum(jnp.max(s_c, axis=-1, keepdims=True),
                                     jnp.max(s_w, axis=-1, keepdims=True)), sk)
        e_c = jnp.exp2(s_c - mx)
        e_w = jnp.exp2(s_w - mx)
        den = (jnp.sum(e_c, axis=-1, keepdims=True) + jnp.sum(e_w, axis=-1, keepdims=True)
               + jnp.exp2(sk - mx))
        o2 = (_dot(e_c.astype(BF16), v_ref[pl.ds(n_lat, n_ctx), slot])
              + _dot(e_w.astype(BF16), v_ref[pl.ds(ws, span), slot])) * (1.0 / den)
        out = o2[:rows] + pltpu.roll(o2[rows:], C_HEAD_DIM, 1)
        for j, p in enumerate(pairs):
            o_ref[:, p] = out[j * tq:(j + 1) * tq].astype(BF16)


def _window_attention(q, qs, k, v, sink, n_lat):
    m, c_q = q.shape
    tq = WIN_Q_TILE
    q_spec = pl.BlockSpec((tq, c_q), lambda b, s: (b, 0))
    kv_spec = pl.BlockSpec(k.shape, lambda b, s: (0, 0), pipeline_mode=pl.Buffered(1))
    return pl.pallas_call(
        functools.partial(_window_attn_kernel, n_lat=n_lat, n_ctx=m - n_lat),
        grid_spec=pltpu.PrefetchScalarGridSpec(
            num_scalar_prefetch=1,
            grid=(m // tq,),
            in_specs=[q_spec, q_spec, kv_spec, kv_spec],
            out_specs=q_spec),
        out_shape=jax.ShapeDtypeStruct((m, c_q), BF16),
        name="window_attention",
        compiler_params=_params("parallel"),
    )(sink, q, qs, k, v)


def _rglru_kernel(*refs, reverse, n_tiles, n_lat_tiles):
    if reverse:
        (xp_ref, x_ref, xn_ref, cw_ref, cb_ref, w_ref, rb_ref, ib_ref, lam_ref, hf_ref, gr_ref,
         out_ref, xe_scr, a_scr, u_scr, h_scr, hb_scr) = refs
    else:
        (xp_ref, x_ref, xn_ref, cw_ref, cb_ref, w_ref, rb_ref, ib_ref, lam_ref,
         out_ref, xe_scr, a_scr, u_scr, h_scr) = refs
    tm, bw = x_ref.shape
    pid = pl.program_id(0)
    tile = (n_tiles - 1 - pid) if reverse else (pid + n_lat_tiles) % n_tiles
    seq_start = (tile == 0) | (tile == n_lat_tiles)
    seq_end = (tile == n_lat_tiles - 1) | (tile == n_tiles - 1)

    @pl.when(pid == 0)
    def _():
        h_scr[...] = jnp.zeros(h_scr.shape, F32)

    xe_scr[0:SUBLANES, :] = jnp.where(seq_start, 0.0, xp_ref[...])
    xe_scr[SUBLANES:SUBLANES + tm, :] = x_ref[...]
    xe_scr[SUBLANES + tm:, :] = jnp.where(seq_end, 0.0, xn_ref[...])
    left = CONV_W // 2
    xc = xe_scr[SUBLANES - left:SUBLANES - left + tm, :] * cw_ref[0:1, :]
    for j in range(1, CONV_W):
        s0 = SUBLANES - left + j
        xc = xc + xe_scr[s0:s0 + tm, :] * cw_ref[j:j + 1, :]
    xc = xc + cb_ref[...]

    blk = bw // B_HEADS
    for hd in range(B_HEADS):
        cols = slice(hd * blk, (hd + 1) * blk)
        xh = xc[:, cols]
        zz = _dot(xh.astype(BF16), w_ref[0, hd])
        r = jax.nn.sigmoid(zz[:, :blk] + rb_ref[0, :, cols])
        gi = jax.nn.sigmoid(zz[:, blk:] + ib_ref[0, :, cols])
        lam = lam_ref[0, :, cols]
        log_sig = -(jnp.maximum(-lam, 0.0) + jnp.log1p(jnp.exp(-jnp.abs(lam))))
        log_a = RG_C * r * log_sig
        th = jnp.tanh(log_a)
        a_scr[:, cols] = jnp.exp(log_a)
        u_scr[:, cols] = jnp.sqrt(-2.0 * th / (1.0 - th)) * (gi * xh)

    dst = hb_scr if reverse else out_ref

    def body(j, h):
        t = (tm - 1 - j) if reverse else j
        h = a_scr[pl.ds(t, 1), :] * h + u_scr[pl.ds(t, 1), :]
        dst[pl.ds(t, 1), :] = h
        return h

    h_scr[...] = lax.fori_loop(0, tm, body, h_scr[...], unroll=8)

    if reverse:
        gr = gr_ref[...]
        cdf = 0.5 * (1.0 + jnp.tanh(0.7978845608028654 * (gr + 0.044715 * (gr * gr * gr))))
        out_ref[...] = ((hf_ref[...] + hb_scr[...]) * (gr * cdf)).astype(BF16)


def _rglru(xr, gr, conv_w, conv_b, wcat, rgate_b, igate_b, lam, n_lat):
    m, bw = xr.shape
    tm = ROW_TILE
    n_tiles, n_lat_tiles = m // tm, n_lat // tm
    per_tile = tm // SUBLANES
    n_sub = m // SUBLANES
    blk = bw // B_HEADS

    def run(reverse, extra_in):
        d = 1 if reverse else 0
        if reverse:
            tile = lambda i: n_tiles - 1 - i
        else:
            tile = lambda i: (i + n_lat_tiles) % n_tiles
        row = pl.BlockSpec((tm, bw), lambda i: (tile(i), 0))
        in_specs = [pl.BlockSpec((SUBLANES, bw), lambda i: (jnp.maximum(tile(i) * per_tile - 1, 0), 0)),
                    row,
                    pl.BlockSpec((SUBLANES, bw), lambda i: (jnp.minimum((tile(i) + 1) * per_tile, n_sub - 1), 0)),
                    _full_spec((CONV_W, bw)), _full_spec((1, bw)),
                    pl.BlockSpec((1, B_HEADS, blk, 2 * blk), lambda i: (d, 0, 0, 0)),
                    pl.BlockSpec((1, 1, bw), lambda i: (d, 0, 0)),
                    pl.BlockSpec((1, 1, bw), lambda i: (d, 0, 0)),
                    pl.BlockSpec((1, 1, bw), lambda i: (d, 0, 0))] + [row] * len(extra_in)
        scratch = [pltpu.VMEM((tm + 2 * SUBLANES, bw), F32), pltpu.VMEM((tm, bw), F32),
                   pltpu.VMEM((tm, bw), F32), pltpu.VMEM((1, bw), F32)]
        if reverse:
            scratch.append(pltpu.VMEM((tm, bw), F32))
        return pl.pallas_call(
            functools.partial(_rglru_kernel, reverse=reverse, n_tiles=n_tiles, n_lat_tiles=n_lat_tiles),
            grid=(n_tiles,),
            in_specs=in_specs,
            out_specs=row,
            out_shape=jax.ShapeDtypeStruct((m, bw), BF16 if reverse else F32),
            scratch_shapes=scratch,
            name="rglru_reverse" if reverse else "rglru_forward",
            compiler_params=_params("arbitrary"),
        )(xr, xr, xr, conv_w, conv_b.reshape(1, bw), wcat, rgate_b.reshape(2, 1, bw),
          igate_b.reshape(2, 1, bw), lam.reshape(2, 1, bw), *extra_in)

    hf = run(False, ())
    return run(True, (hf, gr))


def _post_mixer_kernel(*refs, n_att):
    att_refs = refs[:n_att]
    wout_refs = refs[n_att:2 * n_att]
    (x_ref, mod_ref, g2_ref, rwh_ref, rwl_ref, rb_ref,
     x1_ref, h2_ref, eidx_ref, rank_ref, gate_ref, cnt_ref, cnt_scr) = refs[2 * n_att:]
    tm = x_ref.shape[0]

    @pl.when(pl.program_id(0) == 0)
    def _():
        cnt_scr[...] = jnp.zeros(cnt_scr.shape, F32)

    o = _dot(att_refs[0][...], wout_refs[0][...])
    for a_ref, w_ref in zip(att_refs[1:], wout_refs[1:]):
        o = o + _dot(a_ref[...], w_ref[...])
    x1 = x_ref[...] + mod_ref[0, 2:3, :] * o
    x1_ref[...] = x1
    h2 = _modulate(x1, g2_ref[...], mod_ref[0, 3:4, :], mod_ref[0, 4:5, :])
    h2_ref[...] = h2

    hh = h2.astype(BF16)
    hl = (h2 - hh.astype(F32)).astype(BF16)
    both = _dot(hh, rwl_ref[...])
    logits = both[:, :LANES] + _dot(hl, rwh_ref[...]) + both[:, LANES:]
    scores = jax.nn.sigmoid(logits)
    sel = scores + rb_ref[...]
    lane = lax.broadcasted_iota(I32, (tm, LANES), 1)
    picked = jnp.zeros((tm, LANES), jnp.bool_)
    idxs, vals = [], []
    for _ in range(TOP_K):
        mx = jnp.max(sel, axis=-1, keepdims=True)
        idx = jnp.min(jnp.where(sel == mx, lane, LANES), axis=-1, keepdims=True)
        hit = lane == idx
        vals.append(jnp.sum(jnp.where(hit, scores, 0.0), axis=-1, keepdims=True))
        idxs.append(idx)
        sel = jnp.where(hit, -3e38, sel)
        picked = picked | hit
    total = vals[0]
    for v in vals[1:]:
        total = total + v

    pick_f = picked.astype(F32)
    lower = (lax.broadcasted_iota(I32, (tm, tm), 0) > lax.broadcasted_iota(I32, (tm, tm), 1)).astype(BF16)
    rank_dense = _dot(lower, pick_f.astype(BF16)) + cnt_scr[...]
    cnt_scr[...] = cnt_scr[...] + jnp.sum(pick_f, axis=0, keepdims=True)
    cnt_ref[...] = cnt_scr[...]

    eidx = jnp.zeros((tm, LANES), I32)
    rank = jnp.zeros((tm, LANES), I32)
    gate = jnp.zeros((tm, LANES), F32)
    for k in range(TOP_K):
        rk = jnp.sum(jnp.where(lane == idxs[k], rank_dense, 0.0), axis=-1, keepdims=True)
        eidx = jnp.where(lane == k, idxs[k], eidx)
        rank = jnp.where(lane == k, rk.astype(I32), rank)
        gate = jnp.where(lane == k, ROUTED_SCALE * vals[k] / total, gate)
    eidx_ref[...] = eidx
    rank_ref[...] = rank
    gate_ref[...] = gate


def _post_mixer(atts, wouts, x, mod, g2, rwh, rwl, rb, n_lat):
    m, d = x.shape
    tm = ROW_TILE
    n_att = len(atts)
    in_specs = ([_row_spec(tm, a.shape[1]) for a in atts] + [_full_spec(w.shape) for w in wouts]
                + [_row_spec(tm, d), _mod_spec(d, n_lat // tm), _full_spec((1, d)),
                   _full_spec(rwh.shape), _full_spec(rwl.shape), _full_spec((1, LANES))])
    return pl.pallas_call(
        functools.partial(_post_mixer_kernel, n_att=n_att),
        grid=(m // tm,),
        in_specs=in_specs,
        out_specs=[_row_spec(tm, d), _row_spec(tm, d), _row_spec(tm, LANES), _row_spec(tm, LANES),
                   _row_spec(tm, LANES), _full_spec((1, LANES))],
        out_shape=[jax.ShapeDtypeStruct((m, d), F32), jax.ShapeDtypeStruct((m, d), F32),
                   jax.ShapeDtypeStruct((m, LANES), I32), jax.ShapeDtypeStruct((m, LANES), I32),
                   jax.ShapeDtypeStruct((m, LANES), F32), jax.ShapeDtypeStruct((1, LANES), F32)],
        scratch_shapes=[pltpu.VMEM((1, LANES), F32)],
        name="post_mixer",
        compiler_params=_params("arbitrary"),
    )(*atts, *wouts, x, mod, g2.reshape(1, d), rwh, rwl, rb)


def _dispatch_kernel(dest_ref, h2_ref, xs_in, xs_out, sem):
    del xs_in
    tm = h2_ref.shape[0]

    def issue(r, c):
        for k in range(TOP_K):
            pltpu.make_async_copy(h2_ref.at[pl.ds(r, 1)],
                                  xs_out.at[pl.ds(dest_ref[0, 0, r * TOP_K + k], 1)], sem).start()
        return c
    lax.fori_loop(0, tm, issue, 0)
    for _ in range(TOP_K):
        pltpu.make_async_copy(h2_ref, xs_out.at[pl.ds(0, tm)], sem).wait()


def _dispatch(dest, h2, xs):
    m, d = h2.shape
    tm = ROW_TILE
    return pl.pallas_call(
        _dispatch_kernel,
        grid=(m // tm,),
        in_specs=[pl.BlockSpec((1, 1, tm * TOP_K), lambda i: (i, 0, 0), memory_space=pltpu.SMEM),
                  _row_spec(tm, d), pl.BlockSpec(memory_space=pl.ANY)],
        out_specs=pl.BlockSpec(memory_space=pl.ANY),
        out_shape=jax.ShapeDtypeStruct(xs.shape, xs.dtype),
        scratch_shapes=[pltpu.SemaphoreType.DMA],
        input_output_aliases={2: 0},
        name="dispatch",
        compiler_params=_params("arbitrary", disable_bounds_checks=True),
    )(dest.reshape(m // tm, 1, tm * TOP_K), h2, xs)


def _expert_kernel(ps_ref, cnt_ref, xs_hbm, wg_ref, wu_ref, wd_ref, y_hbm,
                   xbuf, ybuf, wgb, wub, wdb, in_sem, out_sem):
    del xs_hbm
    bm = xbuf.shape[1]
    e = pl.program_id(0)
    n_e = pl.num_programs(0)
    base = ps_ref[e]
    nblk = (cnt_ref[e] + bm - 1) // bm

    def rows(j):
        return pl.ds(pl.multiple_of(base + j * bm, bm), bm)

    def in_copy(j, slot):
        return pltpu.make_async_copy(y_hbm.at[rows(j)], xbuf.at[slot], in_sem.at[slot])

    def out_copy(j, slot):
        return pltpu.make_async_copy(ybuf.at[slot], y_hbm.at[rows(j)], out_sem.at[slot])

    def first_copy(ex):
        start = pl.multiple_of(ps_ref[ex], bm)
        return pltpu.make_async_copy(y_hbm.at[pl.ds(start, bm)], xbuf.at[0], in_sem.at[0])

    def drain(n):
        for back in (1, 2):
            @pl.when(n >= back)
            def _():
                out_copy(0, (n - back) % 2).wait()

    @pl.when((e == 0) & (nblk > 0))
    def _():
        first_copy(0).start()

    wgb[...] = wg_ref[0, 0].astype(BF16)
    wub[...] = wu_ref[0, 0].astype(BF16)
    wdb[...] = wd_ref[0, 0].astype(BF16)

    @pl.when(e > 0)
    def _():
        drain((cnt_ref[jnp.maximum(e - 1, 0)] + bm - 1) // bm)

    def block(j, slot):
        in_copy(j, slot).wait()

        @pl.when(j + 1 < nblk)
        def _():
            in_copy(j + 1, 1 - slot).start()

        @pl.when(j >= 2)
        def _():
            out_copy(j - 2, slot).wait()

        xb = xbuf[slot].astype(BF16)
        hm = _silu(_dot(xb, wgb[...])) * _dot(xb, wub[...])
        ybuf[slot] = _dot(hm.astype(BF16), wdb[...])
        out_copy(j, slot).start()

    def pair(jj, c):
        block(2 * jj, 0)

        @pl.when(2 * jj + 1 < nblk)
        def _():
            block(2 * jj + 1, 1)
        return c
    lax.fori_loop(0, (nblk + 1) // 2, pair, 0)

    nxt = jnp.minimum(e + 1, n_e - 1)

    @pl.when((e + 1 < n_e) & (cnt_ref[nxt] > 0))
    def _():
        first_copy(nxt).start()

    @pl.when(e == n_e - 1)
    def _():
        drain(nblk)


def _experts(xs, pad_start, counts, layer, w_gate, w_up, w_down):
    _, n_e, d, d_e = w_gate.shape
    bm = EXPERT_ROWS
    return pl.pallas_call(
        _expert_kernel,
        grid_spec=pltpu.PrefetchScalarGridSpec(
            num_scalar_prefetch=2,
            grid=(n_e,),
            in_specs=[pl.BlockSpec(memory_space=pl.ANY),
                      pl.BlockSpec((1, 1, d, d_e), lambda e, ps, cn: (layer, e, 0, 0)),
                      pl.BlockSpec((1, 1, d, d_e), lambda e, ps, cn: (layer, e, 0, 0)),
                      pl.BlockSpec((1, 1, d_e, d), lambda e, ps, cn: (layer, e, 0, 0))],
            out_specs=pl.BlockSpec(memory_space=pl.ANY),
            scratch_shapes=[pltpu.VMEM((2, bm, d), F32), pltpu.VMEM((2, bm, d), F32),
                            pltpu.VMEM((d, d_e), BF16), pltpu.VMEM((d, d_e), BF16),
                            pltpu.VMEM((d_e, d), BF16),
                            pltpu.SemaphoreType.DMA((2,)), pltpu.SemaphoreType.DMA((2,))]),
        out_shape=jax.ShapeDtypeStruct(xs.shape, xs.dtype),
        input_output_aliases={2: 0},
        name="experts",
        compiler_params=_params("arbitrary"),
    )(pad_start, counts, xs, w_gate, w_up, w_down)


def _combine_kernel(*refs, final, n_steps):
    if final:
        (dest_ref, next_dest_ref, y_hbm, x1_ref, h2_ref, gate_ref, mod_ref, sg_ref, su_ref, sd_ref,
         fg_ref, o_ref, ybuf, routed_scr, sem) = refs
    else:
        (dest_ref, next_dest_ref, y_hbm, x1_ref, h2_ref, gate_ref, mod_ref, sg_ref, su_ref, sd_ref,
         o_ref, ybuf, routed_scr, sem) = refs
    tc = x1_ref.shape[0]
    i = pl.program_id(0)
    slot = i % 2

    def issue_row(d_ref, r, s):
        for k in range(TOP_K):
            pltpu.make_async_copy(y_hbm.at[pl.ds(d_ref[0, 0, r * TOP_K + k], 1)],
                                  ybuf.at[s, k, pl.ds(r, 1)], sem.at[s]).start()

    def wait_tile(s):
        for k in range(TOP_K):
            pltpu.make_async_copy(y_hbm.at[pl.ds(0, tc)], ybuf.at[s, k], sem.at[s]).wait()

    @pl.when(i == 0)
    def _():
        def first(r, c):
            issue_row(dest_ref, r, 0)
            return c
        lax.fori_loop(0, tc, first, 0)

    wait_tile(slot)

    def body(g, c):
        r0 = pl.multiple_of(g * SUBLANES, SUBLANES)
        for rr in range(SUBLANES):
            issue_row(next_dest_ref, r0 + rr, 1 - slot)
        gate = gate_ref[pl.ds(r0, SUBLANES), :]
        acc = gate[:, 0:1] * ybuf[slot, 0, pl.ds(r0, SUBLANES), :]
        for k in range(1, TOP_K):
            acc = acc + gate[:, k:k + 1] * ybuf[slot, k, pl.ds(r0, SUBLANES), :]
        routed_scr[pl.ds(r0, SUBLANES), :] = acc
        return c
    lax.fori_loop(0, tc // SUBLANES, body, 0)

    @pl.when(i == n_steps - 1)
    def _():
        wait_tile(1 - slot)

    hb = h2_ref[...].astype(BF16)
    shared = _dot((_silu(_dot(hb, sg_ref[...])) * _dot(hb, su_ref[...])).astype(BF16), sd_ref[...])
    x2 = x1_ref[...] + mod_ref[0, 5:6, :] * (routed_scr[...] + shared)
    if final:
        ms = jnp.mean(x2 * x2, axis=-1, keepdims=True)
        x2 = x2 * lax.rsqrt(ms + NORM_EPS) * fg_ref[...]
    o_ref[...] = x2


def _combine(dest, y, x1, h2, gate, mod, sg, su, sd, final_g, n_lat):
    m, d = x1.shape
    tc = COMBINE_TILE
    n_steps = m // tc
    final = final_g is not None
    dest3 = dest.reshape(n_steps, 1, tc * TOP_K)
    in_specs = [pl.BlockSpec((1, 1, tc * TOP_K), lambda i: (i, 0, 0), memory_space=pltpu.SMEM),
                pl.BlockSpec((1, 1, tc * TOP_K), lambda i: (jnp.minimum(i + 1, n_steps - 1), 0, 0),
                             memory_space=pltpu.SMEM),
                pl.BlockSpec(memory_space=pl.ANY),
                _row_spec(tc, d), _row_spec(tc, d), _row_spec(tc, LANES), _mod_spec(d, n_lat // tc),
                _full_spec(sg.shape), _full_spec(su.shape), _full_spec(sd.shape)]
    args = [dest3, dest3, y, x1, h2, gate, mod, sg, su, sd]
    if final:
        in_specs.append(_full_spec((1, d)))
        args.append(final_g.reshape(1, d))
    return pl.pallas_call(
        functools.partial(_combine_kernel, final=final, n_steps=n_steps),
        grid=(n_steps,),
        in_specs=in_specs,
        out_specs=_row_spec(tc, d),
        out_shape=jax.ShapeDtypeStruct((m, d), F32),
        scratch_shapes=[pltpu.VMEM((2, TOP_K, tc, d), F32), pltpu.VMEM((tc, d), F32),
                        pltpu.SemaphoreType.DMA((2,))],
        name="combine",
        compiler_params=_params("arbitrary", disable_bounds_checks=True),
    )(*args)


def _sorted_rows(m, n_experts, bm):
    return -(-(m * TOP_K + n_experts * (bm - 1)) // bm) * bm


def _routing_tables(eidx, rank, counts, bm):
    padded = (counts + bm - 1) // bm * bm
    pad_start = jnp.cumsum(padded) - padded
    experts = jnp.arange(counts.shape[0], dtype=I32)
    start_of_pick = jnp.sum(jnp.where(eidx[:, :, None] == experts, pad_start, 0), axis=-1)
    return pad_start.astype(I32), (start_of_pick + rank).astype(I32)


def kernel(x, c, ctx, c_ctx, w_mod, b_mod, norm1_g, norm2_g, final_norm_g, a_w_in, a_w_out, a_q_norm,
           a_k_norm, b_conv_w, b_conv_b, b_rgate_w, b_rgate_b, b_igate_w, b_igate_b, b_lambda, c_w_in,
           c_w_out, c_sink, moe_router_w, moe_router_b, moe_w_gate, moe_w_up, moe_w_down,
           moe_shared_gate, moe_shared_up, moe_shared_down):
    batch, n_lat, d = x.shape
    n_ctx = ctx.shape[1]
    depth = w_mod.shape[0]
    n_experts = moe_router_w.shape[2]
    assert batch == 1 and n_ctx == ROW_TILE and n_lat % ROW_TILE == 0
    assert n_lat >= WIN_Q_TILE + 2 * WINDOW and n_experts <= LANES

    tk = min(ATT_KV_CHUNK, n_lat)
    xs = jnp.concatenate([x[0], ctx[0]], axis=0)
    mods = _ada_params(jnp.stack([c[0], c_ctx]), w_mod, b_mod).reshape(depth, 2, N_MOD, d)
    tables_a = _rope_tables(n_lat, n_ctx, A_HEAD_DIM)
    tables_c = _rope_tables(n_lat, n_ctx, C_HEAD_DIM)

    pad_e = LANES - n_experts
    rw = jnp.pad(moe_router_w, ((0, 0), (0, 0), (0, pad_e)))
    rwh = rw.astype(BF16)
    rwl = jnp.concatenate([rwh, (rw - rwh.astype(F32)).astype(BF16)], axis=-1)
    rb = jnp.pad(moe_router_b, ((0, 0), (0, pad_e)), constant_values=NEG_INF).reshape(depth, 1, LANES)
    sorted_x = jnp.zeros((_sorted_rows(n_lat + n_ctx, n_experts, EXPERT_ROWS), d), F32)

    for layer in range(depth):
        i = layer // 2
        mod = mods[layer]
        if layer % 2 == 0:
            q, k, v, xr, gr = _in_proj_even(xs, norm1_g[layer], mod, a_w_in[i].astype(BF16), tables_a,
                                            a_q_norm[i], a_k_norm[i], n_lat)
            att = _dense_attention(q, k, v, n_lat, tk)
            wcat = jnp.concatenate([b_rgate_w[i], b_igate_w[i]], axis=-1).astype(BF16)
            rec = _rglru(xr, gr, b_conv_w[i], b_conv_b[i], wcat, b_rgate_b[i], b_igate_b[i],
                         b_lambda[i], n_lat)
            w_out = a_w_out[i].astype(BF16)
            a_q = att.shape[1]
            atts, wouts = (att, rec), (w_out[:a_q], w_out[a_q:])
        else:
            q, qs, k, v = _in_proj_odd(xs, norm1_g[layer], mod, c_w_in[i].astype(BF16), tables_c, n_lat)
            att = _window_attention(q, qs, k, v, c_sink[i], n_lat)
            atts, wouts = (att,), (c_w_out[i].astype(BF16),)
        x1, h2, eidx, rank, gate, cnt = _post_mixer(atts, wouts, xs, mod, norm2_g[layer],
                                                    rwh[layer], rwl[layer], rb[layer], n_lat)
        counts = cnt[0, :n_experts].astype(I32)
        pad_start, dest = _routing_tables(eidx[:, :TOP_K], rank[:, :TOP_K], counts, EXPERT_ROWS)
        sorted_x = _dispatch(dest, h2, sorted_x)
        sorted_x = _experts(sorted_x, pad_start, counts, layer, moe_w_gate, moe_w_up, moe_w_down)
        xs = _combine(dest, sorted_x, x1, h2, gate, mod, moe_shared_gate[layer].astype(BF16),
                      moe_shared_up[layer].astype(BF16), moe_shared_down[layer].astype(BF16),
                      final_norm_g if layer == depth - 1 else None, n_lat)
    return xs[:n_lat].reshape(batch, n_lat, d)
```

```python
import functools

import jax
import jax.numpy as jnp
from jax import lax
from jax.experimental import pallas as pl
from jax.experimental.pallas import tpu as pltpu

F32 = jnp.float32
BF16 = jnp.bfloat16
I32 = jnp.int32

NORM_EPS = 1e-6
ROPE_THETA = 10000.0
GRID_W = 64
N_MOD = 6
A_HEAD_DIM = 128
A_GROUP = 4
B_HEADS = 8
CONV_W = 4
RG_C = 8.0
C_HEAD_DIM = 64
C_GROUP = 8
WINDOW = 128
TOP_K = 8
ROUTED_SCALE = 2.5
NEG_INF = -1e30
LOG2_E = 1.4426950408889634

LANES = 128
SUBLANES = 8
ROW_TILE = 256
ATT_Q_TILE = 256
ATT_KV_CHUNK = 2048
WIN_Q_TILE = 128
EXPERT_ROWS = 256
COMBINE_TILE = 128
ADA_COLS = 512
BLOCK_DMA_QUEUE = 1
VMEM_LIMIT = 48 * 1024 * 1024


def _params(*sem, **kw):
    return pltpu.CompilerParams(dimension_semantics=sem, vmem_limit_bytes=VMEM_LIMIT, **kw)


def _modulate(xf, g, shift, scale):
    ms = jnp.mean(xf * xf, axis=-1, keepdims=True)
    y = xf * lax.rsqrt(ms + NORM_EPS) * g
    return y * (1.0 + scale) + shift


def _silu(x):
    return x * jax.nn.sigmoid(x)


def _dot(a, b):
    return jnp.dot(a, b, preferred_element_type=F32)


def _dot_nt(a, b):
    return lax.dot_general(a, b, (((1,), (1,)), ((), ())), preferred_element_type=F32)


def _ada_kernel(cond_ref, w_ref, b_ref, o_ref):
    tn = w_ref.shape[2]
    for cnd in range(2):
        s = _silu(cond_ref[cnd])
        for j in range(tn // LANES):
            cols = slice(j * LANES, (j + 1) * LANES)
            acc = jnp.sum(w_ref[0, :, cols] * s, axis=0, keepdims=True)
            o_ref[0, cnd:cnd + 1, cols] = acc + b_ref[0, :, cols]


def _ada_params(cond, w_mod, b_mod):
    depth, d, n = w_mod.shape
    cond_b = jnp.broadcast_to(cond[:, :, None], (2, d, LANES))
    return pl.pallas_call(
        _ada_kernel,
        grid=(depth, n // ADA_COLS),
        in_specs=[pl.BlockSpec((2, d, LANES), lambda l, j: (0, 0, 0)),
                  pl.BlockSpec((1, d, ADA_COLS), lambda l, j: (l, 0, j)),
                  pl.BlockSpec((1, 1, ADA_COLS), lambda l, j: (l, 0, j))],
        out_specs=pl.BlockSpec((1, 2, ADA_COLS), lambda l, j: (l, 0, j)),
        out_shape=jax.ShapeDtypeStruct((depth, 2, n), F32),
        name="ada_params",
        compiler_params=_params("parallel", "parallel"),
    )(cond_b, w_mod, b_mod.reshape(depth, 1, n))


def _rope_tables(n_lat, n_ctx, head_dim):
    n_rows = n_lat // GRID_W
    rows = jnp.repeat(jnp.arange(n_rows, dtype=F32), GRID_W)
    cols = jnp.tile(jnp.arange(GRID_W, dtype=F32), n_rows)
    d_axis = head_dim // 2
    inv = ROPE_THETA ** (-jnp.arange(0, d_axis, 2, dtype=F32) / d_axis)
    ar = rows[:, None] * inv
    ac = cols[:, None] * inv
    ang = jnp.concatenate([ar, ar, ac, ac], axis=-1)
    ang = jnp.tile(ang, (1, LANES // head_dim))
    chunk = head_dim // 4
    even = (jnp.arange(LANES) // chunk) % 2 == 0
    cos, sin = jnp.cos(ang), jnp.sin(ang)
    sa = jnp.where(even, -sin, 0.0)
    sb = jnp.where(even, 0.0, sin)
    pad = ((0, n_ctx), (0, 0))
    return (jnp.pad(cos, pad, constant_values=1.0), jnp.pad(sa, pad), jnp.pad(sb, pad))


def _rope(y, cos, sa, sb, chunk):
    return y * cos + pltpu.roll(y, LANES - chunk, 1) * sa + pltpu.roll(y, chunk, 1) * sb


def _in_proj_even_kernel(x_ref, g_ref, mod_ref, w_ref, cos_ref, sa_ref, sb_ref, qn_ref, kn_ref,
                         q_ref, k_ref, v_ref, xr_ref, gr_ref):
    h = _modulate(x_ref[...], g_ref[...], mod_ref[0, 0:1, :], mod_ref[0, 1:2, :])
    z = _dot(h.astype(BF16), w_ref[...])
    cos, sa, sb = cos_ref[...], sa_ref[...], sb_ref[...]
    a_q, a_kv, b_w = q_ref.shape[1], k_ref.shape[1], xr_ref.shape[1]

    def norm_rope(zh, gain):
        ms = jnp.mean(zh * zh, axis=-1, keepdims=True)
        return _rope(zh * lax.rsqrt(ms + NORM_EPS) * gain, cos, sa, sb, A_HEAD_DIM // 4)

    scale = A_HEAD_DIM ** -0.5 * LOG2_E
    for hd in range(a_q // LANES):
        cols = slice(hd * LANES, (hd + 1) * LANES)
        q_ref[:, cols] = (norm_rope(z[:, cols], qn_ref[...]) * scale).astype(BF16)
    for hd in range(a_kv // LANES):
        cols = slice(hd * LANES, (hd + 1) * LANES)
        k_ref[:, cols] = norm_rope(z[:, a_q + hd * LANES:a_q + (hd + 1) * LANES], kn_ref[...]).astype(BF16)
    v_ref[...] = z[:, a_q + a_kv:a_q + 2 * a_kv].astype(BF16)
    xr_ref[...] = z[:, a_q + 2 * a_kv:a_q + 2 * a_kv + b_w]
    gr_ref[...] = z[:, a_q + 2 * a_kv + b_w:]


def _row_spec(tm, n):
    return pl.BlockSpec((tm, n), lambda i: (i, 0))


def _full_spec(shape):
    nd = len(shape)
    return pl.BlockSpec(shape, lambda i: (0,) * nd)


def _mod_spec(d, n_lat_tiles):
    return pl.BlockSpec((1, N_MOD, d), lambda i: (jnp.where(i >= n_lat_tiles, 1, 0), 0, 0))


def _in_proj_even(x, g, mod, w, tables, qn, kn, n_lat):
    m, d = x.shape
    tm = ROW_TILE
    a_q = d // 2
    a_kv = a_q // A_GROUP
    b_w = d // 2
    cos, sa, sb = tables
    return pl.pallas_call(
        _in_proj_even_kernel,
        grid=(m // tm,),
        in_specs=[_row_spec(tm, d), _full_spec((1, d)), _mod_spec(d, n_lat // tm), _full_spec(w.shape),
                  _row_spec(tm, LANES), _row_spec(tm, LANES), _row_spec(tm, LANES),
                  _full_spec((1, LANES)), _full_spec((1, LANES))],
        out_specs=[_row_spec(tm, a_q), _row_spec(tm, a_kv), _row_spec(tm, a_kv),
                   _row_spec(tm, b_w), _row_spec(tm, b_w)],
        out_shape=[jax.ShapeDtypeStruct((m, a_q), BF16), jax.ShapeDtypeStruct((m, a_kv), BF16),
                   jax.ShapeDtypeStruct((m, a_kv), BF16), jax.ShapeDtypeStruct((m, b_w), F32),
                   jax.ShapeDtypeStruct((m, b_w), F32)],
        name="in_proj_even",
        compiler_params=_params("parallel"),
    )(x, g.reshape(1, d), mod, w, cos, sa, sb, qn.reshape(1, LANES), kn.reshape(1, LANES))


def _in_proj_odd_kernel(x_ref, g_ref, mod_ref, w_ref, cos_ref, sa_ref, sb_ref,
                        q_ref, qs_ref, k_ref, v_ref):
    h = _modulate(x_ref[...], g_ref[...], mod_ref[0, 0:1, :], mod_ref[0, 1:2, :])
    z = _dot(h.astype(BF16), w_ref[...])
    cos, sa, sb = cos_ref[...], sa_ref[...], sb_ref[...]
    tm = x_ref.shape[0]
    c_q = q_ref.shape[1]
    c_kv = k_ref.shape[1] // 2
    scale = C_HEAD_DIM ** -0.5 * LOG2_E
    for j in range(c_q // LANES):
        cols = slice(j * LANES, (j + 1) * LANES)
        qj = _rope(z[:, cols], cos, sa, sb, C_HEAD_DIM // 4) * scale
        q_ref[:, cols] = qj.astype(BF16)
        qs_ref[:, cols] = pltpu.roll(qj, C_HEAD_DIM, 1).astype(BF16)
    lo = lax.broadcasted_iota(I32, (tm, LANES), 1) < C_HEAD_DIM

    def expand(pair, out_ref, j):
        swapped = pltpu.roll(pair, C_HEAD_DIM, 1)
        out_ref[:, 2 * j * LANES:(2 * j + 1) * LANES] = jnp.where(lo, pair, 0.0).astype(BF16)
        out_ref[:, (2 * j + 1) * LANES:(2 * j + 2) * LANES] = jnp.where(lo, swapped, 0.0).astype(BF16)

    for j in range(c_kv // LANES):
        k0 = c_q + j * LANES
        v0 = c_q + c_kv + j * LANES
        expand(_rope(z[:, k0:k0 + LANES], cos, sa, sb, C_HEAD_DIM // 4), k_ref, j)
        expand(z[:, v0:v0 + LANES], v_ref, j)


def _in_proj_odd(x, g, mod, w, tables, n_lat):
    m, d = x.shape
    tm = ROW_TILE
    c_q = d
    c_kv = d // C_GROUP
    cos, sa, sb = tables
    kv_shape = jax.ShapeDtypeStruct((m, 2 * c_kv), BF16)
    return pl.pallas_call(
        _in_proj_odd_kernel,
        grid=(m // tm,),
        in_specs=[_row_spec(tm, d), _full_spec((1, d)), _mod_spec(d, n_lat // tm), _full_spec(w.shape),
                  _row_spec(tm, LANES), _row_spec(tm, LANES), _row_spec(tm, LANES)],
        out_specs=[_row_spec(tm, c_q)] * 2 + [_row_spec(tm, 2 * c_kv)] * 2,
        out_shape=[jax.ShapeDtypeStruct((m, c_q), BF16)] * 2 + [kv_shape, kv_shape],
        name="in_proj_odd",
        compiler_params=_params("parallel"),
    )(x, g.reshape(1, d), mod, w, cos, sa, sb)


def _dense_attn_kernel(q_ref, k_ref, v_ref, o_ref, m_ref, l_ref, acc_ref, *, n_lat, n_ctx, tk):
    tq = q_ref.shape[0]
    is_lat = pl.program_id(1) < n_lat // tq
    q = jnp.concatenate([q_ref[:, g * LANES:(g + 1) * LANES] for g in range(A_GROUP)], axis=0)
    m_ref[...] = jnp.full(m_ref.shape, NEG_INF, F32)
    l_ref[...] = jnp.zeros(l_ref.shape, F32)
    acc_ref[...] = jnp.zeros(acc_ref.shape, F32)

    def step(kc, vc):
        s = _dot_nt(q, kc)
        m_old = m_ref[...]
        m_new = jnp.maximum(m_old, jnp.max(s, axis=-1, keepdims=True))
        alpha = jnp.exp2(m_old - m_new)
        p = jnp.exp2(s - m_new)
        l_ref[...] = alpha * l_ref[...] + jnp.sum(p, axis=-1, keepdims=True)
        acc_ref[...] = alpha * acc_ref[...] + _dot(p.astype(BF16), vc)
        m_ref[...] = m_new

    @pl.when(is_lat)
    def _():
        def body(i, carry):
            start = pl.multiple_of(i * tk, tk)
            step(k_ref[pl.ds(start, tk), :], v_ref[pl.ds(start, tk), :])
            return carry
        lax.fori_loop(0, n_lat // tk, body, 0, unroll=4)

    step(k_ref[pl.ds(n_lat, n_ctx), :], v_ref[pl.ds(n_lat, n_ctx), :])
    out = acc_ref[...] / l_ref[...]
    for g in range(A_GROUP):
        o_ref[:, g * LANES:(g + 1) * LANES] = out[g * tq:(g + 1) * tq].astype(BF16)


def _dense_attention(q, k, v, n_lat, tk):
    m, a_q = q.shape
    n_kv = k.shape[1] // LANES
    tq = ATT_Q_TILE
    gw = A_GROUP * LANES
    rows = A_GROUP * tq
    return pl.pallas_call(
        functools.partial(_dense_attn_kernel, n_lat=n_lat, n_ctx=m - n_lat, tk=tk),
        grid=(n_kv, m // tq),
        in_specs=[pl.BlockSpec((tq, gw), lambda kh, i: (i, kh)),
                  pl.BlockSpec((m, LANES), lambda kh, i: (0, kh)),
                  pl.BlockSpec((m, LANES), lambda kh, i: (0, kh))],
        out_specs=pl.BlockSpec((tq, gw), lambda kh, i: (i, kh)),
        out_shape=jax.ShapeDtypeStruct((m, a_q), BF16),
        scratch_shapes=[pltpu.VMEM((rows, 1), F32), pltpu.VMEM((rows, 1), F32),
                        pltpu.VMEM((rows, LANES), F32)],
        name="dense_attention",
        compiler_params=_params("parallel", "parallel"),
    )(q, k, v)


def _window_attn_kernel(sink_ref, q_ref, qs_ref, k_ref, v_ref, o_ref, *, n_lat, n_ctx):
    tq = q_ref.shape[0]
    n_pairs = C_GROUP // 2
    n_kv = k_ref.shape[1] // LANES
    span = tq + 2 * WINDOW
    b = pl.program_id(0)
    is_lat = b < n_lat // tq
    ws = pl.multiple_of(jnp.clip((b - 1) * tq, 0, n_lat - span), tq)
    off = jnp.where(is_lat, ws - b * tq, 4 * span)
    rel = (lax.broadcasted_iota(I32, (tq, span), 1) - lax.broadcasted_iota(I32, (tq, span), 0)) + off
    valid = jnp.concatenate([jnp.abs(rel) <= WINDOW] * C_GROUP, axis=0)
    rows = n_pairs * tq
    for g in range(n_kv):
        slot = slice(g * LANES, (g + 1) * LANES)
        pairs = [slice((g * n_pairs + j) * LANES, (g * n_pairs + j + 1) * LANES) for j in range(n_pairs)]
        q2 = jnp.concatenate([q_ref[:, p] for p in pairs] + [qs_ref[:, p] for p in pairs], axis=0)
        sk = jnp.concatenate(
            [jnp.full((tq, 1), sink_ref[g * C_GROUP + 2 * j + hi] * LOG2_E, F32)
             for hi in range(2) for j in range(n_pairs)], axis=0)
        s_c = _dot_nt(q2, k_ref[pl.ds(n_lat, n_ctx), slot])
        s_w = jnp.where(valid, _dot_nt(q2, k_ref[pl.ds(ws, span), slot]), NEG_INF)
        mx = jnp.maximum(jnp.maximum(jnp.max(s_c, axis=-1, keepdims=True),
                                     jnp.max(s_w, axis=-1, keepdims=True)), sk)
        e_c = jnp.exp2(s_c - mx)
        e_w = jnp.exp2(s_w - mx)
        den = (jnp.sum(e_c, axis=-1, keepdims=True) + jnp.sum(e_w, axis=-1, keepdims=True)
               + jnp.exp2(sk - mx))
        o2 = (_dot(e_c.astype(BF16), v_ref[pl.ds(n_lat, n_ctx), slot])
              + _dot(e_w.astype(BF16), v_ref[pl.ds(ws, span), slot])) * (1.0 / den)
        out = o2[:rows] + pltpu.roll(o2[rows:], C_HEAD_DIM, 1)
        for j, p in enumerate(pairs):
            o_ref[:, p] = out[j * tq:(j + 1) * tq].astype(BF16)


def _window_attention(q, qs, k, v, sink, n_lat):
    m, c_q = q.shape
    tq = WIN_Q_TILE
    q_spec = pl.BlockSpec((tq, c_q), lambda b, s: (b, 0))
    kv_spec = pl.BlockSpec(k.shape, lambda b, s: (0, 0), pipeline_mode=pl.Buffered(1))
    return pl.pallas_call(
        functools.partial(_window_attn_kernel, n_lat=n_lat, n_ctx=m - n_lat),
        grid_spec=pltpu.PrefetchScalarGridSpec(
            num_scalar_prefetch=1,
            grid=(m // tq,),
            in_specs=[q_spec, q_spec, kv_spec, kv_spec],
            out_specs=q_spec),
        out_shape=jax.ShapeDtypeStruct((m, c_q), BF16),
        name="window_attention",
        compiler_params=_params("parallel"),
    )(sink, q, qs, k, v)


def _rglru_kernel(*refs, reverse, n_tiles, n_lat_tiles):
    if reverse:
        (xp_ref, x_ref, xn_ref, cw_ref, cb_ref, w_ref, rb_ref, ib_ref, lam_ref, hf_ref, gr_ref,
         out_ref, xe_scr, a_scr, u_scr, h_scr, hb_scr) = refs
    else:
        (xp_ref, x_ref, xn_ref, cw_ref, cb_ref, w_ref, rb_ref, ib_ref, lam_ref,
         out_ref, xe_scr, a_scr, u_scr, h_scr) = refs
    tm, bw = x_ref.shape
    pid = pl.program_id(0)
    tile = (n_tiles - 1 - pid) if reverse else (pid + n_lat_tiles) % n_tiles
    seq_start = (tile == 0) | (tile == n_lat_tiles)
    seq_end = (tile == n_lat_tiles - 1) | (tile == n_tiles - 1)

    @pl.when(pid == 0)
    def _():
        h_scr[...] = jnp.zeros(h_scr.shape, F32)

    xe_scr[0:SUBLANES, :] = jnp.where(seq_start, 0.0, xp_ref[...])
    xe_scr[SUBLANES:SUBLANES + tm, :] = x_ref[...]
    xe_scr[SUBLANES + tm:, :] = jnp.where(seq_end, 0.0, xn_ref[...])
    left = CONV_W // 2
    xc = xe_scr[SUBLANES - left:SUBLANES - left + tm, :] * cw_ref[0:1, :]
    for j in range(1, CONV_W):
        s0 = SUBLANES - left + j
        xc = xc + xe_scr[s0:s0 + tm, :] * cw_ref[j:j + 1, :]
    xc = xc + cb_ref[...]

    blk = bw // B_HEADS
    for hd in range(B_HEADS):
        cols = slice(hd * blk, (hd + 1) * blk)
        xh = xc[:, cols]
        zz = _dot(xh.astype(BF16), w_ref[0, hd])
        r = jax.nn.sigmoid(zz[:, :blk] + rb_ref[0, :, cols])
        gi = jax.nn.sigmoid(zz[:, blk:] + ib_ref[0, :, cols])
        lam = lam_ref[0, :, cols]
        log_sig = -(jnp.maximum(-lam, 0.0) + jnp.log1p(jnp.exp(-jnp.abs(lam))))
        log_a = RG_C * r * log_sig
        th = jnp.tanh(log_a)
        a_scr[:, cols] = jnp.exp(log_a)
        u_scr[:, cols] = jnp.sqrt(-2.0 * th / (1.0 - th)) * (gi * xh)

    dst = hb_scr if reverse else out_ref

    def body(j, h):
        t = (tm - 1 - j) if reverse else j
        h = a_scr[pl.ds(t, 1), :] * h + u_scr[pl.ds(t, 1), :]
        dst[pl.ds(t, 1), :] = h
        return h

    h_scr[...] = lax.fori_loop(0, tm, body, h_scr[...], unroll=8)

    if reverse:
        gr = gr_ref[...]
        cdf = 0.5 * (1.0 + jnp.tanh(0.7978845608028654 * (gr + 0.044715 * (gr * gr * gr))))
        out_ref[...] = ((hf_ref[...] + hb_scr[...]) * (gr * cdf)).astype(BF16)


def _rglru(xr, gr, conv_w, conv_b, wcat, rgate_b, igate_b, lam, n_lat):
    m, bw = xr.shape
    tm = ROW_TILE
    n_tiles, n_lat_tiles = m // tm, n_lat // tm
    per_tile = tm // SUBLANES
    n_sub = m // SUBLANES
    blk = bw // B_HEADS

    def run(reverse, extra_in):
        d = 1 if reverse else 0
        if reverse:
            tile = lambda i: n_tiles - 1 - i
        else:
            tile = lambda i: (i + n_lat_tiles) % n_tiles
        row = pl.BlockSpec((tm, bw), lambda i: (tile(i), 0))
        in_specs = [pl.BlockSpec((SUBLANES, bw), lambda i: (jnp.maximum(tile(i) * per_tile - 1, 0), 0)),
                    row,
                    pl.BlockSpec((SUBLANES, bw), lambda i: (jnp.minimum((tile(i) + 1) * per_tile, n_sub - 1), 0)),
                    _full_spec((CONV_W, bw)), _full_spec((1, bw)),
                    pl.BlockSpec((1, B_HEADS, blk, 2 * blk), lambda i: (d, 0, 0, 0)),
                    pl.BlockSpec((1, 1, bw), lambda i: (d, 0, 0)),
                    pl.BlockSpec((1, 1, bw), lambda i: (d, 0, 0)),
                    pl.BlockSpec((1, 1, bw), lambda i: (d, 0, 0))] + [row] * len(extra_in)
        scratch = [pltpu.VMEM((tm + 2 * SUBLANES, bw), F32), pltpu.VMEM((tm, bw), F32),
                   pltpu.VMEM((tm, bw), F32), pltpu.VMEM((1, bw), F32)]
        if reverse:
            scratch.append(pltpu.VMEM((tm, bw), F32))
        return pl.pallas_call(
            functools.partial(_rglru_kernel, reverse=reverse, n_tiles=n_tiles, n_lat_tiles=n_lat_tiles),
            grid=(n_tiles,),
            in_specs=in_specs,
            out_specs=row,
            out_shape=jax.ShapeDtypeStruct((m, bw), BF16 if reverse else F32),
            scratch_shapes=scratch,
            name="rglru_reverse" if reverse else "rglru_forward",
            compiler_params=_params("arbitrary"),
        )(xr, xr, xr, conv_w, conv_b.reshape(1, bw), wcat, rgate_b.reshape(2, 1, bw),
          igate_b.reshape(2, 1, bw), lam.reshape(2, 1, bw), *extra_in)

    hf = run(False, ())
    return run(True, (hf, gr))


def _post_mixer_kernel(*refs, n_att):
    att_refs = refs[:n_att]
    wout_refs = refs[n_att:2 * n_att]
    (x_ref, mod_ref, g2_ref, rwh_ref, rwl_ref, rb_ref,
     x1_ref, h2_ref, eidx_ref, rank_ref, gate_ref, cnt_ref, cnt_scr) = refs[2 * n_att:]
    tm = x_ref.shape[0]

    @pl.when(pl.program_id(0) == 0)
    def _():
        cnt_scr[...] = jnp.zeros(cnt_scr.shape, F32)

    o = _dot(att_refs[0][...], wout_refs[0][...])
    for a_ref, w_ref in zip(att_refs[1:], wout_refs[1:]):
        o = o + _dot(a_ref[...], w_ref[...])
    x1 = x_ref[...] + mod_ref[0, 2:3, :] * o
    x1_ref[...] = x1
    h2 = _modulate(x1, g2_ref[...], mod_ref[0, 3:4, :], mod_ref[0, 4:5, :])
    h2_ref[...] = h2

    hh = h2.astype(BF16)
    hl = (h2 - hh.astype(F32)).astype(BF16)
    both = _dot(hh, rwl_ref[...])
    logits = both[:, :LANES] + _dot(hl, rwh_ref[...]) + both[:, LANES:]
    scores = jax.nn.sigmoid(logits)
    sel = scores + rb_ref[...]
    lane = lax.broadcasted_iota(I32, (tm, LANES), 1)
    picked = jnp.zeros((tm, LANES), jnp.bool_)
    idxs, vals = [], []
    for _ in range(TOP_K):
        mx = jnp.max(sel, axis=-1, keepdims=True)
        idx = jnp.min(jnp.where(sel == mx, lane, LANES), axis=-1, keepdims=True)
        hit = lane == idx
        vals.append(jnp.sum(jnp.where(hit, scores, 0.0), axis=-1, keepdims=True))
        idxs.append(idx)
        sel = jnp.where(hit, -3e38, sel)
        picked = picked | hit
    total = vals[0]
    for v in vals[1:]:
        total = total + v

    pick_f = picked.astype(F32)
    lower = (lax.broadcasted_iota(I32, (tm, tm), 0) > lax.broadcasted_iota(I32, (tm, tm), 1)).astype(BF16)
    rank_dense = _dot(lower, pick_f.astype(BF16)) + cnt_scr[...]
    cnt_scr[...] = cnt_scr[...] + jnp.sum(pick_f, axis=0, keepdims=True)
    cnt_ref[...] = cnt_scr[...]

    eidx = jnp.zeros((tm, LANES), I32)
    rank = jnp.zeros((tm, LANES), I32)
    gate = jnp.zeros((tm, LANES), F32)
    for k in range(TOP_K):
        rk = jnp.sum(jnp.where(lane == idxs[k], rank_dense, 0.0), axis=-1, keepdims=True)
        eidx = jnp.where(lane == k, idxs[k], eidx)
        rank = jnp.where(lane == k, rk.astype(I32), rank)
        gate = jnp.where(lane == k, ROUTED_SCALE * vals[k] / total, gate)
    eidx_ref[...] = eidx
    rank_ref[...] = rank
    gate_ref[...] = gate


def _post_mixer(atts, wouts, x, mod, g2, rwh, rwl, rb, n_lat):
    m, d = x.shape
    tm = ROW_TILE
    n_att = len(atts)
    in_specs = ([_row_spec(tm, a.shape[1]) for a in atts] + [_full_spec(w.shape) for w in wouts]
                + [_row_spec(tm, d), _mod_spec(d, n_lat // tm), _full_spec((1, d)),
                   _full_spec(rwh.shape), _full_spec(rwl.shape), _full_spec((1, LANES))])
    return pl.pallas_call(
        functools.partial(_post_mixer_kernel, n_att=n_att),
        grid=(m // tm,),
        in_specs=in_specs,
        out_specs=[_row_spec(tm, d), _row_spec(tm, d), _row_spec(tm, LANES), _row_spec(tm, LANES),
                   _row_spec(tm, LANES), _full_spec((1, LANES))],
        out_shape=[jax.ShapeDtypeStruct((m, d), F32), jax.ShapeDtypeStruct((m, d), F32),
                   jax.ShapeDtypeStruct((m, LANES), I32), jax.ShapeDtypeStruct((m, LANES), I32),
                   jax.ShapeDtypeStruct((m, LANES), F32), jax.ShapeDtypeStruct((1, LANES), F32)],
        scratch_shapes=[pltpu.VMEM((1, LANES), F32)],
        name="post_mixer",
        compiler_params=_params("arbitrary"),
    )(*atts, *wouts, x, mod, g2.reshape(1, d), rwh, rwl, rb)


def _dispatch_kernel(dest_ref, h2_ref, xs_in, xs_out, sem):
    del xs_in
    tm = h2_ref.shape[0]

    def issue(r, c):
        for k in range(TOP_K):
            pltpu.make_async_copy(h2_ref.at[pl.ds(r, 1)],
                                  xs_out.at[pl.ds(dest_ref[0, 0, r * TOP_K + k], 1)],
                                  sem).start(priority=k % 2)
        return c
    lax.fori_loop(0, tm, issue, 0)
    for _ in range(TOP_K):
        pltpu.make_async_copy(h2_ref, xs_out.at[pl.ds(0, tm)], sem).wait()


def _dispatch(dest, h2, xs):
    m, d = h2.shape
    tm = ROW_TILE
    return pl.pallas_call(
        _dispatch_kernel,
        grid=(m // tm,),
        in_specs=[pl.BlockSpec((1, 1, tm * TOP_K), lambda i: (i, 0, 0), memory_space=pltpu.SMEM),
                  _row_spec(tm, d), pl.BlockSpec(memory_space=pl.ANY)],
        out_specs=pl.BlockSpec(memory_space=pl.ANY),
        out_shape=jax.ShapeDtypeStruct(xs.shape, xs.dtype),
        scratch_shapes=[pltpu.SemaphoreType.DMA],
        input_output_aliases={2: 0},
        name="dispatch",
        compiler_params=_params("arbitrary", disable_bounds_checks=True),
    )(dest.reshape(m // tm, 1, tm * TOP_K), h2, xs)


def _expert_kernel(ps_ref, cnt_ref, xs_hbm, wg_ref, wu_ref, wd_ref, y_hbm,
                   xbuf, ybuf, wgb, wub, wdb, in_sem, out_sem):
    del xs_hbm
    bm = xbuf.shape[1]
    e = pl.program_id(0)
    n_e = pl.num_programs(0)
    base = ps_ref[e]
    nblk = (cnt_ref[e] + bm - 1) // bm

    def rows(j):
        return pl.ds(pl.multiple_of(base + j * bm, bm), bm)

    def in_copy(j, slot):
        return pltpu.make_async_copy(y_hbm.at[rows(j)], xbuf.at[slot], in_sem.at[slot])

    def out_copy(j, slot):
        return pltpu.make_async_copy(ybuf.at[slot], y_hbm.at[rows(j)], out_sem.at[slot])

    def first_copy(ex):
        start = pl.multiple_of(ps_ref[ex], bm)
        return pltpu.make_async_copy(y_hbm.at[pl.ds(start, bm)], xbuf.at[0], in_sem.at[0])

    def drain(n):
        for back in (1, 2):
            @pl.when(n >= back)
            def _():
                out_copy(0, (n - back) % 2).wait()

    @pl.when((e == 0) & (nblk > 0))
    def _():
        first_copy(0).start(priority=BLOCK_DMA_QUEUE)

    wgb[...] = wg_ref[0, 0].astype(BF16)
    wub[...] = wu_ref[0, 0].astype(BF16)
    wdb[...] = wd_ref[0, 0].astype(BF16)

    @pl.when(e > 0)
    def _():
        drain((cnt_ref[jnp.maximum(e - 1, 0)] + bm - 1) // bm)

    def block(j, slot):
        in_copy(j, slot).wait()

        @pl.when(j + 1 < nblk)
        def _():
            in_copy(j + 1, 1 - slot).start(priority=BLOCK_DMA_QUEUE)

        @pl.when(j >= 2)
        def _():
            out_copy(j - 2, slot).wait()

        xb = xbuf[slot].astype(BF16)
        hm = _silu(_dot(xb, wgb[...])) * _dot(xb, wub[...])
        ybuf[slot] = _dot(hm.astype(BF16), wdb[...])
        out_copy(j, slot).start(priority=BLOCK_DMA_QUEUE)

    def pair(jj, c):
        block(2 * jj, 0)

        @pl.when(2 * jj + 1 < nblk)
        def _():
            block(2 * jj + 1, 1)
        return c
    lax.fori_loop(0, (nblk + 1) // 2, pair, 0)

    nxt = jnp.minimum(e + 1, n_e - 1)

    @pl.when((e + 1 < n_e) & (cnt_ref[nxt] > 0))
    def _():
        first_copy(nxt).start(priority=BLOCK_DMA_QUEUE)

    @pl.when(e == n_e - 1)
    def _():
        drain(nblk)


def _experts(xs, pad_start, counts, layer, w_gate, w_up, w_down):
    _, n_e, d, d_e = w_gate.shape
    bm = EXPERT_ROWS
    return pl.pallas_call(
        _expert_kernel,
        grid_spec=pltpu.PrefetchScalarGridSpec(
            num_scalar_prefetch=2,
            grid=(n_e,),
            in_specs=[pl.BlockSpec(memory_space=pl.ANY),
                      pl.BlockSpec((1, 1, d, d_e), lambda e, ps, cn: (layer, e, 0, 0)),
                      pl.BlockSpec((1, 1, d, d_e), lambda e, ps, cn: (layer, e, 0, 0)),
                      pl.BlockSpec((1, 1, d_e, d), lambda e, ps, cn: (layer, e, 0, 0))],
            out_specs=pl.BlockSpec(memory_space=pl.ANY),
            scratch_shapes=[pltpu.VMEM((2, bm, d), F32), pltpu.VMEM((2, bm, d), F32),
                            pltpu.VMEM((d, d_e), BF16), pltpu.VMEM((d, d_e), BF16),
                            pltpu.VMEM((d_e, d), BF16),
                            pltpu.SemaphoreType.DMA((2,)), pltpu.SemaphoreType.DMA((2,))]),
        out_shape=jax.ShapeDtypeStruct(xs.shape, xs.dtype),
        input_output_aliases={2: 0},
        name="experts",
        compiler_params=_params("arbitrary"),
    )(pad_start, counts, xs, w_gate, w_up, w_down)


def _combine_kernel(*refs, final, n_steps):
    if final:
        (dest_ref, next_dest_ref, y_hbm, x1_ref, h2_ref, gate_ref, mod_ref, sg_ref, su_ref, sd_ref,
         fg_ref, o_ref, ybuf, routed_scr, sem) = refs
    else:
        (dest_ref, next_dest_ref, y_hbm, x1_ref, h2_ref, gate_ref, mod_ref, sg_ref, su_ref, sd_ref,
         o_ref, ybuf, routed_scr, sem) = refs
    tc = x1_ref.shape[0]
    i = pl.program_id(0)
    slot = i % 2

    def issue_row(d_ref, r, s):
        for k in range(TOP_K):
            pltpu.make_async_copy(y_hbm.at[pl.ds(d_ref[0, 0, r * TOP_K + k], 1)],
                                  ybuf.at[s, k, pl.ds(r, 1)], sem.at[s]).start(priority=k % 2)

    def wait_tile(s):
        for k in range(TOP_K):
            pltpu.make_async_copy(y_hbm.at[pl.ds(0, tc)], ybuf.at[s, k], sem.at[s]).wait()

    @pl.when(i == 0)
    def _():
        def first(r, c):
            issue_row(dest_ref, r, 0)
            return c
        lax.fori_loop(0, tc, first, 0)

    wait_tile(slot)

    def body(g, c):
        r0 = pl.multiple_of(g * SUBLANES, SUBLANES)
        for rr in range(SUBLANES):
            issue_row(next_dest_ref, r0 + rr, 1 - slot)
        gate = gate_ref[pl.ds(r0, SUBLANES), :]
        acc = gate[:, 0:1] * ybuf[slot, 0, pl.ds(r0, SUBLANES), :]
        for k in range(1, TOP_K):
            acc = acc + gate[:, k:k + 1] * ybuf[slot, k, pl.ds(r0, SUBLANES), :]
        routed_scr[pl.ds(r0, SUBLANES), :] = acc
        return c
    lax.fori_loop(0, tc // SUBLANES, body, 0)

    @pl.when(i == n_steps - 1)
    def _():
        wait_tile(1 - slot)

    hb = h2_ref[...].astype(BF16)
    shared = _dot((_silu(_dot(hb, sg_ref[...])) * _dot(hb, su_ref[...])).astype(BF16), sd_ref[...])
    x2 = x1_ref[...] + mod_ref[0, 5:6, :] * (routed_scr[...] + shared)
    if final:
        ms = jnp.mean(x2 * x2, axis=-1, keepdims=True)
        x2 = x2 * lax.rsqrt(ms + NORM_EPS) * fg_ref[...]
    o_ref[...] = x2


def _combine(dest, y, x1, h2, gate, mod, sg, su, sd, final_g, n_lat):
    m, d = x1.shape
    tc = COMBINE_TILE
    n_steps = m // tc
    final = final_g is not None
    dest3 = dest.reshape(n_steps, 1, tc * TOP_K)
    in_specs = [pl.BlockSpec((1, 1, tc * TOP_K), lambda i: (i, 0, 0), memory_space=pltpu.SMEM),
                pl.BlockSpec((1, 1, tc * TOP_K), lambda i: (jnp.minimum(i + 1, n_steps - 1), 0, 0),
                             memory_space=pltpu.SMEM),
                pl.BlockSpec(memory_space=pl.ANY),
                _row_spec(tc, d), _row_spec(tc, d), _row_spec(tc, LANES), _mod_spec(d, n_lat // tc),
                _full_spec(sg.shape), _full_spec(su.shape), _full_spec(sd.shape)]
    args = [dest3, dest3, y, x1, h2, gate, mod, sg, su, sd]
    if final:
        in_specs.append(_full_spec((1, d)))
        args.append(final_g.reshape(1, d))
    return pl.pallas_call(
        functools.partial(_combine_kernel, final=final, n_steps=n_steps),
        grid=(n_steps,),
        in_specs=in_specs,
        out_specs=_row_spec(tc, d),
        out_shape=jax.ShapeDtypeStruct((m, d), F32),
        scratch_shapes=[pltpu.VMEM((2, TOP_K, tc, d), F32), pltpu.VMEM((tc, d), F32),
                        pltpu.SemaphoreType.DMA((2,))],
        name="combine",
        compiler_params=_params("arbitrary", disable_bounds_checks=True),
    )(*args)


def _sorted_rows(m, n_experts, bm):
    return -(-(m * TOP_K + n_experts * (bm - 1)) // bm) * bm


def _routing_tables(eidx, rank, counts, bm):
    padded = (counts + bm - 1) // bm * bm
    pad_start = jnp.cumsum(padded) - padded
    experts = jnp.arange(counts.shape[0], dtype=I32)
    start_of_pick = jnp.sum(jnp.where(eidx[:, :, None] == experts, pad_start, 0), axis=-1)
    return pad_start.astype(I32), (start_of_pick + rank).astype(I32)


def kernel(x, c, ctx, c_ctx, w_mod, b_mod, norm1_g, norm2_g, final_norm_g, a_w_in, a_w_out, a_q_norm,
           a_k_norm, b_conv_w, b_conv_b, b_rgate_w, b_rgate_b, b_igate_w, b_igate_b, b_lambda, c_w_in,
           c_w_out, c_sink, moe_router_w, moe_router_b, moe_w_gate, moe_w_up, moe_w_down,
           moe_shared_gate, moe_shared_up, moe_shared_down):
    batch, n_lat, d = x.shape
    n_ctx = ctx.shape[1]
    depth = w_mod.shape[0]
    n_experts = moe_router_w.shape[2]
    assert batch == 1 and n_ctx == ROW_TILE and n_lat % ROW_TILE == 0
    assert n_lat >= WIN_Q_TILE + 2 * WINDOW and n_experts <= LANES

    tk = min(ATT_KV_CHUNK, n_lat)
    xs = jnp.concatenate([x[0], ctx[0]], axis=0)
    mods = _ada_params(jnp.stack([c[0], c_ctx]), w_mod, b_mod).reshape(depth, 2, N_MOD, d)
    tables_a = _rope_tables(n_lat, n_ctx, A_HEAD_DIM)
    tables_c = _rope_tables(n_lat, n_ctx, C_HEAD_DIM)

    pad_e = LANES - n_experts
    rw = jnp.pad(moe_router_w, ((0, 0), (0, 0), (0, pad_e)))
    rwh = rw.astype(BF16)
    rwl = jnp.concatenate([rwh, (rw - rwh.astype(F32)).astype(BF16)], axis=-1)
    rb = jnp.pad(moe_router_b, ((0, 0), (0, pad_e)), constant_values=NEG_INF).reshape(depth, 1, LANES)
    sorted_x = jnp.zeros((_sorted_rows(n_lat + n_ctx, n_experts, EXPERT_ROWS), d), F32)

    for layer in range(depth):
        i = layer // 2
        mod = mods[layer]
        if layer % 2 == 0:
            q, k, v, xr, gr = _in_proj_even(xs, norm1_g[layer], mod, a_w_in[i].astype(BF16), tables_a,
                                            a_q_norm[i], a_k_norm[i], n_lat)
            att = _dense_attention(q, k, v, n_lat, tk)
            wcat = jnp.concatenate([b_rgate_w[i], b_igate_w[i]], axis=-1).astype(BF16)
            rec = _rglru(xr, gr, b_conv_w[i], b_conv_b[i], wcat, b_rgate_b[i], b_igate_b[i],
                         b_lambda[i], n_lat)
            w_out = a_w_out[i].astype(BF16)
            a_q = att.shape[1]
            atts, wouts = (att, rec), (w_out[:a_q], w_out[a_q:])
        else:
            q, qs, k, v = _in_proj_odd(xs, norm1_g[layer], mod, c_w_in[i].astype(BF16), tables_c, n_lat)
            att = _window_attention(q, qs, k, v, c_sink[i], n_lat)
            atts, wouts = (att,), (c_w_out[i].astype(BF16),)
        x1, h2, eidx, rank, gate, cnt = _post_mixer(atts, wouts, xs, mod, norm2_g[layer],
                                                    rwh[layer], rwl[layer], rb[layer], n_lat)
        counts = cnt[0, :n_experts].astype(I32)
        pad_start, dest = _routing_tables(eidx[:, :TOP_K], rank[:, :TOP_K], counts, EXPERT_ROWS)
        sorted_x = _dispatch(dest, h2, sorted_x)
        sorted_x = _experts(sorted_x, pad_start, counts, layer, moe_w_gate, moe_w_up, moe_w_down)
        xs = _combine(dest, sorted_x, x1, h2, gate, mod, moe_shared_gate[layer].astype(BF16),
                      moe_shared_up[layer].astype(BF16), moe_shared_down[layer].astype(BF16),
                      final_norm_g if layer == depth - 1 else None, n_lat)
    return xs[:n_lat].reshape(batch, n_lat, d)
```

```python
import functools

import jax
import jax.numpy as jnp
from jax import lax
from jax.experimental import pallas as pl
from jax.experimental.pallas import tpu as pltpu

F32 = jnp.float32
BF16 = jnp.bfloat16
I32 = jnp.int32

NORM_EPS = 1e-6
ROPE_THETA = 10000.0
GRID_W = 64
N_MOD = 6
A_HEAD_DIM = 128
A_GROUP = 4
B_HEADS = 8
CONV_W = 4
RG_C = 8.0
C_HEAD_DIM = 64
C_GROUP = 8
WINDOW = 128
TOP_K = 8
ROUTED_SCALE = 2.5
NEG_INF = -1e30
LOG2_E = 1.4426950408889634

LANES = 128
SUBLANES = 8
ROW_TILE = 256
ATT_Q_TILE = 256
ATT_KV_CHUNK = 2048
WIN_Q_TILE = 128
EXPERT_ROWS = 256
COMBINE_TILE = 128
ADA_COLS = 512
BLOCK_DMA_QUEUE = 0
VMEM_LIMIT = 48 * 1024 * 1024


def _params(*sem, **kw):
    return pltpu.CompilerParams(dimension_semantics=sem, vmem_limit_bytes=VMEM_LIMIT, **kw)


def _modulate(xf, g, shift, scale):
    ms = jnp.mean(xf * xf, axis=-1, keepdims=True)
    y = xf * lax.rsqrt(ms + NORM_EPS) * g
    return y * (1.0 + scale) + shift


def _silu(x):
    return x * jax.nn.sigmoid(x)


def _dot(a, b):
    return jnp.dot(a, b, preferred_element_type=F32)


def _dot_nt(a, b):
    return lax.dot_general(a, b, (((1,), (1,)), ((), ())), preferred_element_type=F32)


def _ada_kernel(cond_ref, w_ref, b_ref, o_ref):
    tn = w_ref.shape[2]
    for cnd in range(2):
        s = _silu(cond_ref[cnd])
        for j in range(tn // LANES):
            cols = slice(j * LANES, (j + 1) * LANES)
            acc = jnp.sum(w_ref[0, :, cols] * s, axis=0, keepdims=True)
            o_ref[0, cnd:cnd + 1, cols] = acc + b_ref[0, :, cols]


def _ada_params(cond, w_mod, b_mod):
    depth, d, n = w_mod.shape
    cond_b = jnp.broadcast_to(cond[:, :, None], (2, d, LANES))
    return pl.pallas_call(
        _ada_kernel,
        grid=(depth, n // ADA_COLS),
        in_specs=[pl.BlockSpec((2, d, LANES), lambda l, j: (0, 0, 0)),
                  pl.BlockSpec((1, d, ADA_COLS), lambda l, j: (l, 0, j)),
                  pl.BlockSpec((1, 1, ADA_COLS), lambda l, j: (l, 0, j))],
        out_specs=pl.BlockSpec((1, 2, ADA_COLS), lambda l, j: (l, 0, j)),
        out_shape=jax.ShapeDtypeStruct((depth, 2, n), F32),
        name="ada_params",
        compiler_params=_params("parallel", "parallel"),
    )(cond_b, w_mod, b_mod.reshape(depth, 1, n))


def _rope_tables(n_lat, n_ctx, head_dim):
    n_rows = n_lat // GRID_W
    rows = jnp.repeat(jnp.arange(n_rows, dtype=F32), GRID_W)
    cols = jnp.tile(jnp.arange(GRID_W, dtype=F32), n_rows)
    d_axis = head_dim // 2
    inv = ROPE_THETA ** (-jnp.arange(0, d_axis, 2, dtype=F32) / d_axis)
    ar = rows[:, None] * inv
    ac = cols[:, None] * inv
    ang = jnp.concatenate([ar, ar, ac, ac], axis=-1)
    ang = jnp.tile(ang, (1, LANES // head_dim))
    chunk = head_dim // 4
    even = (jnp.arange(LANES) // chunk) % 2 == 0
    cos, sin = jnp.cos(ang), jnp.sin(ang)
    sa = jnp.where(even, -sin, 0.0)
    sb = jnp.where(even, 0.0, sin)
    pad = ((0, n_ctx), (0, 0))
    return (jnp.pad(cos, pad, constant_values=1.0), jnp.pad(sa, pad), jnp.pad(sb, pad))


def _rope(y, cos, sa, sb, chunk):
    return y * cos + pltpu.roll(y, LANES - chunk, 1) * sa + pltpu.roll(y, chunk, 1) * sb


def _in_proj_even_kernel(x_ref, g_ref, mod_ref, w_ref, cos_ref, sa_ref, sb_ref, qn_ref, kn_ref,
                         q_ref, k_ref, v_ref, xr_ref, gr_ref):
    h = _modulate(x_ref[...], g_ref[...], mod_ref[0, 0:1, :], mod_ref[0, 1:2, :])
    z = _dot(h.astype(BF16), w_ref[...])
    cos, sa, sb = cos_ref[...], sa_ref[...], sb_ref[...]
    a_q, a_kv, b_w = q_ref.shape[1], k_ref.shape[1], xr_ref.shape[1]

    def norm_rope(zh, gain):
        ms = jnp.mean(zh * zh, axis=-1, keepdims=True)
        return _rope(zh * lax.rsqrt(ms + NORM_EPS) * gain, cos, sa, sb, A_HEAD_DIM // 4)

    scale = A_HEAD_DIM ** -0.5 * LOG2_E
    for hd in range(a_q // LANES):
        cols = slice(hd * LANES, (hd + 1) * LANES)
        q_ref[:, cols] = (norm_rope(z[:, cols], qn_ref[...]) * scale).astype(BF16)
    for hd in range(a_kv // LANES):
        cols = slice(hd * LANES, (hd + 1) * LANES)
        k_ref[:, cols] = norm_rope(z[:, a_q + hd * LANES:a_q + (hd + 1) * LANES], kn_ref[...]).astype(BF16)
    v_ref[...] = z[:, a_q + a_kv:a_q + 2 * a_kv].astype(BF16)
    xr_ref[...] = z[:, a_q + 2 * a_kv:a_q + 2 * a_kv + b_w]
    gr_ref[...] = z[:, a_q + 2 * a_kv + b_w:]


def _row_spec(tm, n):
    return pl.BlockSpec((tm, n), lambda i: (i, 0))


def _full_spec(shape):
    nd = len(shape)
    return pl.BlockSpec(shape, lambda i: (0,) * nd)


def _mod_spec(d, n_lat_tiles):
    return pl.BlockSpec((1, N_MOD, d), lambda i: (jnp.where(i >= n_lat_tiles, 1, 0), 0, 0))


def _in_proj_even(x, g, mod, w, tables, qn, kn, n_lat):
    m, d = x.shape
    tm = ROW_TILE
    a_q = d // 2
    a_kv = a_q // A_GROUP
    b_w = d // 2
    cos, sa, sb = tables
    return pl.pallas_call(
        _in_proj_even_kernel,
        grid=(m // tm,),
        in_specs=[_row_spec(tm, d), _full_spec((1, d)), _mod_spec(d, n_lat // tm), _full_spec(w.shape),
                  _row_spec(tm, LANES), _row_spec(tm, LANES), _row_spec(tm, LANES),
                  _full_spec((1, LANES)), _full_spec((1, LANES))],
        out_specs=[_row_spec(tm, a_q), _row_spec(tm, a_kv), _row_spec(tm, a_kv),
                   _row_spec(tm, b_w), _row_spec(tm, b_w)],
        out_shape=[jax.ShapeDtypeStruct((m, a_q), BF16), jax.ShapeDtypeStruct((m, a_kv), BF16),
                   jax.ShapeDtypeStruct((m, a_kv), BF16), jax.ShapeDtypeStruct((m, b_w), F32),
                   jax.ShapeDtypeStruct((m, b_w), F32)],
        name="in_proj_even",
        compiler_params=_params("parallel"),
    )(x, g.reshape(1, d), mod, w, cos, sa, sb, qn.reshape(1, LANES), kn.reshape(1, LANES))


def _in_proj_odd_kernel(x_ref, g_ref, mod_ref, w_ref, cos_ref, sa_ref, sb_ref,
                        q_ref, qs_ref, k_ref, v_ref):
    h = _modulate(x_ref[...], g_ref[...], mod_ref[0, 0:1, :], mod_ref[0, 1:2, :])
    z = _dot(h.astype(BF16), w_ref[...])
    cos, sa, sb = cos_ref[...], sa_ref[...], sb_ref[...]
    tm = x_ref.shape[0]
    c_q = q_ref.shape[1]
    c_kv = k_ref.shape[1] // 2
    scale = C_HEAD_DIM ** -0.5 * LOG2_E
    for j in range(c_q // LANES):
        cols = slice(j * LANES, (j + 1) * LANES)
        qj = _rope(z[:, cols], cos, sa, sb, C_HEAD_DIM // 4) * scale
        q_ref[:, cols] = qj.astype(BF16)
        qs_ref[:, cols] = pltpu.roll(qj, C_HEAD_DIM, 1).astype(BF16)
    lo = lax.broadcasted_iota(I32, (tm, LANES), 1) < C_HEAD_DIM

    def expand(pair, out_ref, j):
        swapped = pltpu.roll(pair, C_HEAD_DIM, 1)
        out_ref[:, 2 * j * LANES:(2 * j + 1) * LANES] = jnp.where(lo, pair, 0.0).astype(BF16)
        out_ref[:, (2 * j + 1) * LANES:(2 * j + 2) * LANES] = jnp.where(lo, swapped, 0.0).astype(BF16)

    for j in range(c_kv // LANES):
        k0 = c_q + j * LANES
        v0 = c_q + c_kv + j * LANES
        expand(_rope(z[:, k0:k0 + LANES], cos, sa, sb, C_HEAD_DIM // 4), k_ref, j)
        expand(z[:, v0:v0 + LANES], v_ref, j)


def _in_proj_odd(x, g, mod, w, tables, n_lat):
    m, d = x.shape
    tm = ROW_TILE
    c_q = d
    c_kv = d // C_GROUP
    cos, sa, sb = tables
    kv_shape = jax.ShapeDtypeStruct((m, 2 * c_kv), BF16)
    return pl.pallas_call(
        _in_proj_odd_kernel,
        grid=(m // tm,),
        in_specs=[_row_spec(tm, d), _full_spec((1, d)), _mod_spec(d, n_lat // tm), _full_spec(w.shape),
                  _row_spec(tm, LANES), _row_spec(tm, LANES), _row_spec(tm, LANES)],
        out_specs=[_row_spec(tm, c_q)] * 2 + [_row_spec(tm, 2 * c_kv)] * 2,
        out_shape=[jax.ShapeDtypeStruct((m, c_q), BF16)] * 2 + [kv_shape, kv_shape],
        name="in_proj_odd",
        compiler_params=_params("parallel"),
    )(x, g.reshape(1, d), mod, w, cos, sa, sb)


def _dense_attn_kernel(q_ref, k_ref, v_ref, o_ref, m_ref, l_ref, acc_ref, *, n_lat, n_ctx, tk):
    tq = q_ref.shape[0]
    is_lat = pl.program_id(1) < n_lat // tq
    q = jnp.concatenate([q_ref[:, g * LANES:(g + 1) * LANES] for g in range(A_GROUP)], axis=0)
    m_ref[...] = jnp.full(m_ref.shape, NEG_INF, F32)
    l_ref[...] = jnp.zeros(l_ref.shape, F32)
    acc_ref[...] = jnp.zeros(acc_ref.shape, F32)

    def step(kc, vc):
        s = _dot_nt(q, kc)
        m_old = m_ref[...]
        m_new = jnp.maximum(m_old, jnp.max(s, axis=-1, keepdims=True))
        alpha = jnp.exp2(m_old - m_new)
        p = jnp.exp2(s - m_new)
        l_ref[...] = alpha * l_ref[...] + jnp.sum(p, axis=-1, keepdims=True)
        acc_ref[...] = alpha * acc_ref[...] + _dot(p.astype(BF16), vc)
        m_ref[...] = m_new

    @pl.when(is_lat)
    def _():
        def body(i, carry):
            start = pl.multiple_of(i * tk, tk)
            step(k_ref[pl.ds(start, tk), :], v_ref[pl.ds(start, tk), :])
            return carry
        lax.fori_loop(0, n_lat // tk, body, 0, unroll=4)

    step(k_ref[pl.ds(n_lat, n_ctx), :], v_ref[pl.ds(n_lat, n_ctx), :])
    out = acc_ref[...] / l_ref[...]
    for g in range(A_GROUP):
        o_ref[:, g * LANES:(g + 1) * LANES] = out[g * tq:(g + 1) * tq].astype(BF16)


def _dense_attention(q, k, v, n_lat, tk):
    m, a_q = q.shape
    n_kv = k.shape[1] // LANES
    tq = ATT_Q_TILE
    gw = A_GROUP * LANES
    rows = A_GROUP * tq
    return pl.pallas_call(
        functools.partial(_dense_attn_kernel, n_lat=n_lat, n_ctx=m - n_lat, tk=tk),
        grid=(n_kv, m // tq),
        in_specs=[pl.BlockSpec((tq, gw), lambda kh, i: (i, kh)),
                  pl.BlockSpec((m, LANES), lambda kh, i: (0, kh)),
                  pl.BlockSpec((m, LANES), lambda kh, i: (0, kh))],
        out_specs=pl.BlockSpec((tq, gw), lambda kh, i: (i, kh)),
        out_shape=jax.ShapeDtypeStruct((m, a_q), BF16),
        scratch_shapes=[pltpu.VMEM((rows, 1), F32), pltpu.VMEM((rows, 1), F32),
                        pltpu.VMEM((rows, LANES), F32)],
        name="dense_attention",
        compiler_params=_params("parallel", "parallel"),
    )(q, k, v)


def _window_attn_kernel(sink_ref, q_ref, qs_ref, k_ref, v_ref, o_ref, *, n_lat, n_ctx):
    tq = q_ref.shape[0]
    n_pairs = C_GROUP // 2
    n_kv = k_ref.shape[1] // LANES
    span = tq + 2 * WINDOW
    b = pl.program_id(0)
    is_lat = b < n_lat // tq
    ws = pl.multiple_of(jnp.clip(b * tq - WINDOW, 0, n_lat - span), WINDOW)
    off = jnp.where(is_lat, ws - b * tq, 4 * span)
    rel = (lax.broadcasted_iota(I32, (tq, span), 1) - lax.broadcasted_iota(I32, (tq, span), 0)) + off
    valid = jnp.concatenate([jnp.abs(rel) <= WINDOW] * C_GROUP, axis=0)
    rows = n_pairs * tq
    for g in range(n_kv):
        slot = slice(g * LANES, (g + 1) * LANES)
        pairs = [slice((g * n_pairs + j) * LANES, (g * n_pairs + j + 1) * LANES) for j in range(n_pairs)]
        q2 = jnp.concatenate([q_ref[:, p] for p in pairs] + [qs_ref[:, p] for p in pairs], axis=0)
        sk = jnp.concatenate(
            [jnp.full((tq, 1), sink_ref[g * C_GROUP + 2 * j + hi] * LOG2_E, F32)
             for hi in range(2) for j in range(n_pairs)], axis=0)
        s_c = _dot_nt(q2, k_ref[pl.ds(n_lat, n_ctx), slot])
        s_w = jnp.where(valid, _dot_nt(q2, k_ref[pl.ds(ws, span), slot]), NEG_INF)
        mx = jnp.maximum(jnp.maximum(jnp.max(s_c, axis=-1, keepdims=True),
                                     jnp.max(s_w, axis=-1, keepdims=True)), sk)
        e_c = jnp.exp2(s_c - mx)
        e_w = jnp.exp2(s_w - mx)
        den = (jnp.sum(e_c, axis=-1, keepdims=True) + jnp.sum(e_w, axis=-1, keepdims=True)
               + jnp.exp2(sk - mx))
        o2 = (_dot(e_c.astype(BF16), v_ref[pl.ds(n_lat, n_ctx), slot])
              + _dot(e_w.astype(BF16), v_ref[pl.ds(ws, span), slot])) * (1.0 / den)
        out = o2[:rows] + pltpu.roll(o2[rows:], C_HEAD_DIM, 1)
        for j, p in enumerate(pairs):
            o_ref[:, p] = out[j * tq:(j + 1) * tq].astype(BF16)


def _window_attention(q, qs, k, v, sink, n_lat):
    m, c_q = q.shape
    tq = WIN_Q_TILE
    q_spec = pl.BlockSpec((tq, c_q), lambda b, s: (b, 0))
    kv_spec = pl.BlockSpec(k.shape, lambda b, s: (0, 0), pipeline_mode=pl.Buffered(1))
    return pl.pallas_call(
        functools.partial(_window_attn_kernel, n_lat=n_lat, n_ctx=m - n_lat),
        grid_spec=pltpu.PrefetchScalarGridSpec(
            num_scalar_prefetch=1,
            grid=(m // tq,),
            in_specs=[q_spec, q_spec, kv_spec, kv_spec],
            out_specs=q_spec),
        out_shape=jax.ShapeDtypeStruct((m, c_q), BF16),
        name="window_attention",
        compiler_params=_params("parallel"),
    )(sink, q, qs, k, v)


def _rglru_kernel(*refs, reverse, n_tiles, n_lat_tiles):
    if reverse:
        (xp_ref, x_ref, xn_ref, cw_ref, cb_ref, w_ref, rb_ref, ib_ref, lam_ref, hf_ref, gr_ref,
         out_ref, xe_scr, a_scr, u_scr, h_scr, hb_scr) = refs
    else:
        (xp_ref, x_ref, xn_ref, cw_ref, cb_ref, w_ref, rb_ref, ib_ref, lam_ref,
         out_ref, xe_scr, a_scr, u_scr, h_scr) = refs
    tm, bw = x_ref.shape
    pid = pl.program_id(0)
    tile = (n_tiles - 1 - pid) if reverse else (pid + n_lat_tiles) % n_tiles
    seq_start = (tile == 0) | (tile == n_lat_tiles)
    seq_end = (tile == n_lat_tiles - 1) | (tile == n_tiles - 1)

    @pl.when(pid == 0)
    def _():
        h_scr[...] = jnp.zeros(h_scr.shape, F32)

    xe_scr[0:SUBLANES, :] = jnp.where(seq_start, 0.0, xp_ref[...])
    xe_scr[SUBLANES:SUBLANES + tm, :] = x_ref[...]
    xe_scr[SUBLANES + tm:, :] = jnp.where(seq_end, 0.0, xn_ref[...])
    left = CONV_W // 2
    xc = xe_scr[SUBLANES - left:SUBLANES - left + tm, :] * cw_ref[0:1, :]
    for j in range(1, CONV_W):
        s0 = SUBLANES - left + j
        xc = xc + xe_scr[s0:s0 + tm, :] * cw_ref[j:j + 1, :]
    xc = xc + cb_ref[...]

    blk = bw // B_HEADS
    for hd in range(B_HEADS):
        cols = slice(hd * blk, (hd + 1) * blk)
        xh = xc[:, cols]
        zz = _dot(xh.astype(BF16), w_ref[0, hd])
        r = jax.nn.sigmoid(zz[:, :blk] + rb_ref[0, :, cols])
        gi = jax.nn.sigmoid(zz[:, blk:] + ib_ref[0, :, cols])
        lam = lam_ref[0, :, cols]
        log_sig = -(jnp.maximum(-lam, 0.0) + jnp.log1p(jnp.exp(-jnp.abs(lam))))
        log_a = RG_C * r * log_sig
        th = jnp.tanh(log_a)
        a_scr[:, cols] = jnp.exp(log_a)
        u_scr[:, cols] = jnp.sqrt(-2.0 * th / (1.0 - th)) * (gi * xh)

    dst = hb_scr if reverse else out_ref

    def body(j, h):
        t = (tm - 1 - j) if reverse else j
        h = a_scr[pl.ds(t, 1), :] * h + u_scr[pl.ds(t, 1), :]
        dst[pl.ds(t, 1), :] = h
        return h

    h_scr[...] = lax.fori_loop(0, tm, body, h_scr[...], unroll=8)

    if reverse:
        gr = gr_ref[...]
        cdf = 0.5 * (1.0 + jnp.tanh(0.7978845608028654 * (gr + 0.044715 * (gr * gr * gr))))
        out_ref[...] = ((hf_ref[...] + hb_scr[...]) * (gr * cdf)).astype(BF16)


def _rglru(xr, gr, conv_w, conv_b, wcat, rgate_b, igate_b, lam, n_lat):
    m, bw = xr.shape
    tm = ROW_TILE
    n_tiles, n_lat_tiles = m // tm, n_lat // tm
    per_tile = tm // SUBLANES
    n_sub = m // SUBLANES
    blk = bw // B_HEADS

    def run(reverse, extra_in):
        d = 1 if reverse else 0
        if reverse:
            tile = lambda i: n_tiles - 1 - i
        else:
            tile = lambda i: (i + n_lat_tiles) % n_tiles
        row = pl.BlockSpec((tm, bw), lambda i: (tile(i), 0))
        in_specs = [pl.BlockSpec((SUBLANES, bw), lambda i: (jnp.maximum(tile(i) * per_tile - 1, 0), 0)),
                    row,
                    pl.BlockSpec((SUBLANES, bw), lambda i: (jnp.minimum((tile(i) + 1) * per_tile, n_sub - 1), 0)),
                    _full_spec((CONV_W, bw)), _full_spec((1, bw)),
                    pl.BlockSpec((1, B_HEADS, blk, 2 * blk), lambda i: (d, 0, 0, 0)),
                    pl.BlockSpec((1, 1, bw), lambda i: (d, 0, 0)),
                    pl.BlockSpec((1, 1, bw), lambda i: (d, 0, 0)),
                    pl.BlockSpec((1, 1, bw), lambda i: (d, 0, 0))] + [row] * len(extra_in)
        scratch = [pltpu.VMEM((tm + 2 * SUBLANES, bw), F32), pltpu.VMEM((tm, bw), F32),
                   pltpu.VMEM((tm, bw), F32), pltpu.VMEM((1, bw), F32)]
        if reverse:
            scratch.append(pltpu.VMEM((tm, bw), F32))
        return pl.pallas_call(
            functools.partial(_rglru_kernel, reverse=reverse, n_tiles=n_tiles, n_lat_tiles=n_lat_tiles),
            grid=(n_tiles,),
            in_specs=in_specs,
            out_specs=row,
            out_shape=jax.ShapeDtypeStruct((m, bw), BF16 if reverse else F32),
            scratch_shapes=scratch,
            name="rglru_reverse" if reverse else "rglru_forward",
            compiler_params=_params("arbitrary"),
        )(xr, xr, xr, conv_w, conv_b.reshape(1, bw), wcat, rgate_b.reshape(2, 1, bw),
          igate_b.reshape(2, 1, bw), lam.reshape(2, 1, bw), *extra_in)

    hf = run(False, ())
    return run(True, (hf, gr))


def _post_mixer_kernel(*refs, n_att):
    att_refs = refs[:n_att]
    wout_refs = refs[n_att:2 * n_att]
    (x_ref, mod_ref, g2_ref, rwh_ref, rwl_ref, rb_ref,
     x1_ref, h2_ref, eidx_ref, rank_ref, gate_ref, cnt_ref, cnt_scr) = refs[2 * n_att:]
    tm = x_ref.shape[0]

    @pl.when(pl.program_id(0) == 0)
    def _():
        cnt_scr[...] = jnp.zeros(cnt_scr.shape, F32)

    o = _dot(att_refs[0][...], wout_refs[0][...])
    for a_ref, w_ref in zip(att_refs[1:], wout_refs[1:]):
        o = o + _dot(a_ref[...], w_ref[...])
    x1 = x_ref[...] + mod_ref[0, 2:3, :] * o
    x1_ref[...] = x1
    h2 = _modulate(x1, g2_ref[...], mod_ref[0, 3:4, :], mod_ref[0, 4:5, :])
    h2_ref[...] = h2

    hh = h2.astype(BF16)
    hl = (h2 - hh.astype(F32)).astype(BF16)
    both = _dot(hh, rwl_ref[...])
    logits = both[:, :LANES] + _dot(hl, rwh_ref[...]) + both[:, LANES:]
    scores = jax.nn.sigmoid(logits)
    sel = scores + rb_ref[...]
    lane = lax.broadcasted_iota(I32, (tm, LANES), 1)
    picked = jnp.zeros((tm, LANES), jnp.bool_)
    idxs, vals = [], []
    for _ in range(TOP_K):
        mx = jnp.max(sel, axis=-1, keepdims=True)
        idx = jnp.min(jnp.where(sel == mx, lane, LANES), axis=-1, keepdims=True)
        hit = lane == idx
        vals.append(jnp.sum(jnp.where(hit, scores, 0.0), axis=-1, keepdims=True))
        idxs.append(idx)
        sel = jnp.where(hit, -3e38, sel)
        picked = picked | hit
    total = vals[0]
    for v in vals[1:]:
        total = total + v

    pick_f = picked.astype(F32)
    lower = (lax.broadcasted_iota(I32, (tm, tm), 0) > lax.broadcasted_iota(I32, (tm, tm), 1)).astype(BF16)
    rank_dense = _dot(lower, pick_f.astype(BF16)) + cnt_scr[...]
    cnt_scr[...] = cnt_scr[...] + jnp.sum(pick_f, axis=0, keepdims=True)
    cnt_ref[...] = cnt_scr[...]

    eidx = jnp.zeros((tm, LANES), I32)
    rank = jnp.zeros((tm, LANES), I32)
    gate = jnp.zeros((tm, LANES), F32)
    for k in range(TOP_K):
        rk = jnp.sum(jnp.where(lane == idxs[k], rank_dense, 0.0), axis=-1, keepdims=True)
        eidx = jnp.where(lane == k, idxs[k], eidx)
        rank = jnp.where(lane == k, rk.astype(I32), rank)
        gate = jnp.where(lane == k, ROUTED_SCALE * vals[k] / total, gate)
    eidx_ref[...] = eidx
    rank_ref[...] = rank
    gate_ref[...] = gate


def _post_mixer(atts, wouts, x, mod, g2, rwh, rwl, rb, n_lat):
    m, d = x.shape
    tm = ROW_TILE
    n_att = len(atts)
    in_specs = ([_row_spec(tm, a.shape[1]) for a in atts] + [_full_spec(w.shape) for w in wouts]
                + [_row_spec(tm, d), _mod_spec(d, n_lat // tm), _full_spec((1, d)),
                   _full_spec(rwh.shape), _full_spec(rwl.shape), _full_spec((1, LANES))])
    return pl.pallas_call(
        functools.partial(_post_mixer_kernel, n_att=n_att),
        grid=(m // tm,),
        in_specs=in_specs,
        out_specs=[_row_spec(tm, d), _row_spec(tm, d), _row_spec(tm, LANES), _row_spec(tm, LANES),
                   _row_spec(tm, LANES), _full_spec((1, LANES))],
        out_shape=[jax.ShapeDtypeStruct((m, d), F32), jax.ShapeDtypeStruct((m, d), F32),
                   jax.ShapeDtypeStruct((m, LANES), I32), jax.ShapeDtypeStruct((m, LANES), I32),
                   jax.ShapeDtypeStruct((m, LANES), F32), jax.ShapeDtypeStruct((1, LANES), F32)],
        scratch_shapes=[pltpu.VMEM((1, LANES), F32)],
        name="post_mixer",
        compiler_params=_params("arbitrary"),
    )(*atts, *wouts, x, mod, g2.reshape(1, d), rwh, rwl, rb)


def _dispatch_kernel(dest_ref, h2_ref, xs_in, xs_out, sem):
    del xs_in
    tm = h2_ref.shape[0]

    def issue(r, c):
        for k in range(TOP_K):
            pltpu.make_async_copy(h2_ref.at[pl.ds(r, 1)],
                                  xs_out.at[pl.ds(dest_ref[0, 0, r * TOP_K + k], 1)],
                                  sem).start(priority=k % 2)
        return c
    lax.fori_loop(0, tm, issue, 0)
    for _ in range(TOP_K):
        pltpu.make_async_copy(h2_ref, xs_out.at[pl.ds(0, tm)], sem).wait()


def _dispatch(dest, h2, xs):
    m, d = h2.shape
    tm = ROW_TILE
    return pl.pallas_call(
        _dispatch_kernel,
        grid=(m // tm,),
        in_specs=[pl.BlockSpec((1, 1, tm * TOP_K), lambda i: (i, 0, 0), memory_space=pltpu.SMEM),
                  _row_spec(tm, d), pl.BlockSpec(memory_space=pl.ANY)],
        out_specs=pl.BlockSpec(memory_space=pl.ANY),
        out_shape=jax.ShapeDtypeStruct(xs.shape, xs.dtype),
        scratch_shapes=[pltpu.SemaphoreType.DMA],
        input_output_aliases={2: 0},
        name="dispatch",
        compiler_params=_params("arbitrary", disable_bounds_checks=True),
    )(dest.reshape(m // tm, 1, tm * TOP_K), h2, xs)


def _expert_kernel(ps_ref, cnt_ref, xs_hbm, wg_ref, wu_ref, wd_ref, y_hbm,
                   xbuf, ybuf, wgb, wub, wdb, in_sem, out_sem):
    del xs_hbm
    bm = xbuf.shape[1]
    e = pl.program_id(0)
    n_e = pl.num_programs(0)
    base = ps_ref[e]
    nblk = (cnt_ref[e] + bm - 1) // bm
    tail_rows = cnt_ref[e] - (nblk - 1) * bm

    def rows(j):
        return pl.ds(pl.multiple_of(base + j * bm, bm), bm)

    def in_copy(j, slot):
        return pltpu.make_async_copy(y_hbm.at[rows(j)], xbuf.at[slot], in_sem.at[slot])

    def out_copy(j, slot):
        return pltpu.make_async_copy(ybuf.at[slot], y_hbm.at[rows(j)], out_sem.at[slot])

    def first_copy(ex):
        start = pl.multiple_of(ps_ref[ex], bm)
        return pltpu.make_async_copy(y_hbm.at[pl.ds(start, bm)], xbuf.at[0], in_sem.at[0])

    def drain(n):
        for back in (1, 2):
            @pl.when(n >= back)
            def _():
                out_copy(0, (n - back) % 2).wait()

    @pl.when((e == 0) & (nblk > 0))
    def _():
        first_copy(0).start(priority=BLOCK_DMA_QUEUE)

    @pl.when(e == 0)
    def _():
        ybuf[...] = jnp.zeros(ybuf.shape, F32)

    wgb[...] = wg_ref[0, 0].astype(BF16)
    wub[...] = wu_ref[0, 0].astype(BF16)
    wdb[...] = wd_ref[0, 0].astype(BF16)

    @pl.when(e > 0)
    def _():
        drain((cnt_ref[jnp.maximum(e - 1, 0)] + bm - 1) // bm)

    def block(j, slot):
        in_copy(j, slot).wait()

        @pl.when(j + 1 < nblk)
        def _():
            in_copy(j + 1, 1 - slot).start(priority=BLOCK_DMA_QUEUE)

        @pl.when(j >= 2)
        def _():
            out_copy(j - 2, slot).wait()

        def compute(n):
            xb = xbuf[slot, :n].astype(BF16)
            hm = _silu(_dot(xb, wgb[...])) * _dot(xb, wub[...])
            ybuf[slot, :n] = _dot(hm.astype(BF16), wdb[...])

        short = (j == nblk - 1) & (tail_rows <= bm // 2)

        @pl.when(short)
        def _():
            compute(bm // 2)

        @pl.when(jnp.logical_not(short))
        def _():
            compute(bm)

        out_copy(j, slot).start(priority=BLOCK_DMA_QUEUE)

    def pair(jj, c):
        block(2 * jj, 0)

        @pl.when(2 * jj + 1 < nblk)
        def _():
            block(2 * jj + 1, 1)
        return c
    lax.fori_loop(0, (nblk + 1) // 2, pair, 0)

    nxt = jnp.minimum(e + 1, n_e - 1)

    @pl.when((e + 1 < n_e) & (cnt_ref[nxt] > 0))
    def _():
        first_copy(nxt).start(priority=BLOCK_DMA_QUEUE)

    @pl.when(e == n_e - 1)
    def _():
        drain(nblk)


def _experts(xs, pad_start, counts, layer, w_gate, w_up, w_down):
    _, n_e, d, d_e = w_gate.shape
    bm = EXPERT_ROWS
    return pl.pallas_call(
        _expert_kernel,
        grid_spec=pltpu.PrefetchScalarGridSpec(
            num_scalar_prefetch=2,
            grid=(n_e,),
            in_specs=[pl.BlockSpec(memory_space=pl.ANY),
                      pl.BlockSpec((1, 1, d, d_e), lambda e, ps, cn: (layer, e, 0, 0)),
                      pl.BlockSpec((1, 1, d, d_e), lambda e, ps, cn: (layer, e, 0, 0)),
                      pl.BlockSpec((1, 1, d_e, d), lambda e, ps, cn: (layer, e, 0, 0))],
            out_specs=pl.BlockSpec(memory_space=pl.ANY),
            scratch_shapes=[pltpu.VMEM((2, bm, d), F32), pltpu.VMEM((2, bm, d), F32),
                            pltpu.VMEM((d, d_e), BF16), pltpu.VMEM((d, d_e), BF16),
                            pltpu.VMEM((d_e, d), BF16),
                            pltpu.SemaphoreType.DMA((2,)), pltpu.SemaphoreType.DMA((2,))]),
        out_shape=jax.ShapeDtypeStruct(xs.shape, xs.dtype),
        input_output_aliases={2: 0},
        name="experts",
        compiler_params=_params("arbitrary"),
    )(pad_start, counts, xs, w_gate, w_up, w_down)


def _combine_kernel(*refs, final, n_steps):
    if final:
        (dest_ref, next_dest_ref, y_hbm, x1_ref, h2_ref, gate_ref, mod_ref, sg_ref, su_ref, sd_ref,
         fg_ref, o_ref, ybuf, routed_scr, sem) = refs
    else:
        (dest_ref, next_dest_ref, y_hbm, x1_ref, h2_ref, gate_ref, mod_ref, sg_ref, su_ref, sd_ref,
         o_ref, ybuf, routed_scr, sem) = refs
    tc = x1_ref.shape[0]
    i = pl.program_id(0)
    slot = i % 2

    def issue_row(d_ref, r, s):
        for k in range(TOP_K):
            pltpu.make_async_copy(y_hbm.at[pl.ds(d_ref[0, 0, r * TOP_K + k], 1)],
                                  ybuf.at[s, k, pl.ds(r, 1)], sem.at[s]).start(priority=k % 2)

    def wait_tile(s):
        for k in range(TOP_K):
            pltpu.make_async_copy(y_hbm.at[pl.ds(0, tc)], ybuf.at[s, k], sem.at[s]).wait()

    @pl.when(i == 0)
    def _():
        def first(r, c):
            issue_row(dest_ref, r, 0)
            return c
        lax.fori_loop(0, tc, first, 0)

    wait_tile(slot)

    def body(g, c):
        r0 = pl.multiple_of(g * SUBLANES, SUBLANES)
        for rr in range(SUBLANES):
            issue_row(next_dest_ref, r0 + rr, 1 - slot)
        gate = gate_ref[pl.ds(r0, SUBLANES), :]
        acc = gate[:, 0:1] * ybuf[slot, 0, pl.ds(r0, SUBLANES), :]
        for k in range(1, TOP_K):
            acc = acc + gate[:, k:k + 1] * ybuf[slot, k, pl.ds(r0, SUBLANES), :]
        routed_scr[pl.ds(r0, SUBLANES), :] = acc
        return c
    lax.fori_loop(0, tc // SUBLANES, body, 0)

    @pl.when(i == n_steps - 1)
    def _():
        wait_tile(1 - slot)

    hb = h2_ref[...].astype(BF16)
    shared = _dot((_silu(_dot(hb, sg_ref[...])) * _dot(hb, su_ref[...])).astype(BF16), sd_ref[...])
    x2 = x1_ref[...] + mod_ref[0, 5:6, :] * (routed_scr[...] + shared)
    if final:
        ms = jnp.mean(x2 * x2, axis=-1, keepdims=True)
        x2 = x2 * lax.rsqrt(ms + NORM_EPS) * fg_ref[...]
    o_ref[...] = x2


def _combine(dest, y, x1, h2, gate, mod, sg, su, sd, final_g, n_lat):
    m, d = x1.shape
    tc = COMBINE_TILE
    n_steps = m // tc
    final = final_g is not None
    dest3 = dest.reshape(n_steps, 1, tc * TOP_K)
    in_specs = [pl.BlockSpec((1, 1, tc * TOP_K), lambda i: (i, 0, 0), memory_space=pltpu.SMEM),
                pl.BlockSpec((1, 1, tc * TOP_K), lambda i: (jnp.minimum(i + 1, n_steps - 1), 0, 0),
                             memory_space=pltpu.SMEM),
                pl.BlockSpec(memory_space=pl.ANY),
                _row_spec(tc, d), _row_spec(tc, d), _row_spec(tc, LANES), _mod_spec(d, n_lat // tc),
                _full_spec(sg.shape), _full_spec(su.shape), _full_spec(sd.shape)]
    args = [dest3, dest3, y, x1, h2, gate, mod, sg, su, sd]
    if final:
        in_specs.append(_full_spec((1, d)))
        args.append(final_g.reshape(1, d))
    return pl.pallas_call(
        functools.partial(_combine_kernel, final=final, n_steps=n_steps),
        grid=(n_steps,),
        in_specs=in_specs,
        out_specs=_row_spec(tc, d),
        out_shape=jax.ShapeDtypeStruct((m, d), F32),
        scratch_shapes=[pltpu.VMEM((2, TOP_K, tc, d), F32), pltpu.VMEM((tc, d), F32),
                        pltpu.SemaphoreType.DMA((2,))],
        name="combine",
        compiler_params=_params("arbitrary", disable_bounds_checks=True),
    )(*args)


def _sorted_rows(m, n_experts, bm):
    return -(-(m * TOP_K + n_experts * (bm - 1)) // bm) * bm


def _routing_tables(eidx, rank, counts, bm):
    padded = (counts + bm - 1) // bm * bm
    pad_start = jnp.cumsum(padded) - padded
    experts = jnp.arange(counts.shape[0], dtype=I32)
    start_of_pick = jnp.sum(jnp.where(eidx[:, :, None] == experts, pad_start, 0), axis=-1)
    return pad_start.astype(I32), (start_of_pick + rank).astype(I32)


def kernel(x, c, ctx, c_ctx, w_mod, b_mod, norm1_g, norm2_g, final_norm_g, a_w_in, a_w_out, a_q_norm,
           a_k_norm, b_conv_w, b_conv_b, b_rgate_w, b_rgate_b, b_igate_w, b_igate_b, b_lambda, c_w_in,
           c_w_out, c_sink, moe_router_w, moe_router_b, moe_w_gate, moe_w_up, moe_w_down,
           moe_shared_gate, moe_shared_up, moe_shared_down):
    batch, n_lat, d = x.shape
    n_ctx = ctx.shape[1]
    depth = w_mod.shape[0]
    n_experts = moe_router_w.shape[2]
    assert batch == 1 and n_ctx == ROW_TILE and n_lat % ROW_TILE == 0
    assert n_lat >= WIN_Q_TILE + 2 * WINDOW and n_experts <= LANES

    tk = min(ATT_KV_CHUNK, n_lat)
    xs = jnp.concatenate([x[0], ctx[0]], axis=0)
    mods = _ada_params(jnp.stack([c[0], c_ctx]), w_mod, b_mod).reshape(depth, 2, N_MOD, d)
    tables_a = _rope_tables(n_lat, n_ctx, A_HEAD_DIM)
    tables_c = _rope_tables(n_lat, n_ctx, C_HEAD_DIM)

    pad_e = LANES - n_experts
    rw = jnp.pad(moe_router_w, ((0, 0), (0, 0), (0, pad_e)))
    rwh = rw.astype(BF16)
    rwl = jnp.concatenate([rwh, (rw - rwh.astype(F32)).astype(BF16)], axis=-1)
    rb = jnp.pad(moe_router_b, ((0, 0), (0, pad_e)), constant_values=NEG_INF).reshape(depth, 1, LANES)
    sorted_x = jnp.zeros((_sorted_rows(n_lat + n_ctx, n_experts, EXPERT_ROWS), d), F32)

    for layer in range(depth):
        i = layer // 2
        mod = mods[layer]
        if layer % 2 == 0:
            q, k, v, xr, gr = _in_proj_even(xs, norm1_g[layer], mod, a_w_in[i].astype(BF16), tables_a,
                                            a_q_norm[i], a_k_norm[i], n_lat)
            att = _dense_attention(q, k, v, n_lat, tk)
            wcat = jnp.concatenate([b_rgate_w[i], b_igate_w[i]], axis=-1).astype(BF16)
            rec = _rglru(xr, gr, b_conv_w[i], b_conv_b[i], wcat, b_rgate_b[i], b_igate_b[i],
                         b_lambda[i], n_lat)
            w_out = a_w_out[i].astype(BF16)
            a_q = att.shape[1]
            atts, wouts = (att, rec), (w_out[:a_q], w_out[a_q:])
        else:
            q, qs, k, v = _in_proj_odd(xs, norm1_g[layer], mod, c_w_in[i].astype(BF16), tables_c, n_lat)
            att = _window_attention(q, qs, k, v, c_sink[i], n_lat)
            atts, wouts = (att,), (c_w_out[i].astype(BF16),)
        x1, h2, eidx, rank, gate, cnt = _post_mixer(atts, wouts, xs, mod, norm2_g[layer],
                                                    rwh[layer], rwl[layer], rb[layer], n_lat)
        counts = cnt[0, :n_experts].astype(I32)
        pad_start, dest = _routing_tables(eidx[:, :TOP_K], rank[:, :TOP_K], counts, EXPERT_ROWS)
        sorted_x = _dispatch(dest, h2, sorted_x)
        sorted_x = _experts(sorted_x, pad_start, counts, layer, moe_w_gate, moe_w_up, moe_w_down)
        xs = _combine(dest, sorted_x, x1, h2, gate, mod, moe_shared_gate[layer].astype(BF16),
                      moe_shared_up[layer].astype(BF16), moe_shared_down[layer].astype(BF16),
                      final_norm_g if layer == depth - 1 else None, n_lat)
    return xs[:n_lat].reshape(batch, n_lat, d)
```

```python
import functools

import jax
import jax.numpy as jnp
from jax import lax
from jax.experimental import pallas as pl
from jax.experimental.pallas import tpu as pltpu

F32 = jnp.float32
BF16 = jnp.bfloat16
I32 = jnp.int32

NORM_EPS = 1e-6
ROPE_THETA = 10000.0
GRID_W = 64
N_MOD = 6
A_HEAD_DIM = 128
A_GROUP = 4
B_HEADS = 8
CONV_W = 4
RG_C = 8.0
C_HEAD_DIM = 64
C_GROUP = 8
WINDOW = 128
TOP_K = 8
ROUTED_SCALE = 2.5
NEG_INF = -1e30
LOG2_E = 1.4426950408889634

LANES = 128
SUBLANES = 8
ROW_TILE = 256
ATT_Q_TILE = 256
ATT_KV_CHUNK = 2048
WIN_Q_TILE = 128
EXPERT_ROWS = 384
COMBINE_TILE = 128
ADA_COLS = 512
BLOCK_DMA_QUEUE = 0
VMEM_LIMIT = 48 * 1024 * 1024


def _params(*sem, **kw):
    return pltpu.CompilerParams(dimension_semantics=sem, vmem_limit_bytes=VMEM_LIMIT, **kw)


def _modulate(xf, g, shift, scale):
    ms = jnp.mean(xf * xf, axis=-1, keepdims=True)
    y = xf * lax.rsqrt(ms + NORM_EPS) * g
    return y * (1.0 + scale) + shift


def _silu(x):
    return x * jax.nn.sigmoid(x)


def _dot(a, b):
    return jnp.dot(a, b, preferred_element_type=F32)


def _dot_nt(a, b):
    return lax.dot_general(a, b, (((1,), (1,)), ((), ())), preferred_element_type=F32)


def _ada_kernel(cond_ref, w_ref, b_ref, o_ref):
    tn = w_ref.shape[2]
    for cnd in range(2):
        s = _silu(cond_ref[cnd])
        for j in range(tn // LANES):
            cols = slice(j * LANES, (j + 1) * LANES)
            acc = jnp.sum(w_ref[0, :, cols] * s, axis=0, keepdims=True)
            o_ref[0, cnd:cnd + 1, cols] = acc + b_ref[0, :, cols]


def _ada_params(cond, w_mod, b_mod):
    depth, d, n = w_mod.shape
    cond_b = jnp.broadcast_to(cond[:, :, None], (2, d, LANES))
    return pl.pallas_call(
        _ada_kernel,
        grid=(depth, n // ADA_COLS),
        in_specs=[pl.BlockSpec((2, d, LANES), lambda l, j: (0, 0, 0)),
                  pl.BlockSpec((1, d, ADA_COLS), lambda l, j: (l, 0, j)),
                  pl.BlockSpec((1, 1, ADA_COLS), lambda l, j: (l, 0, j))],
        out_specs=pl.BlockSpec((1, 2, ADA_COLS), lambda l, j: (l, 0, j)),
        out_shape=jax.ShapeDtypeStruct((depth, 2, n), F32),
        name="ada_params",
        compiler_params=_params("parallel", "parallel"),
    )(cond_b, w_mod, b_mod.reshape(depth, 1, n))


def _rope_tables(n_lat, n_ctx, head_dim):
    n_rows = n_lat // GRID_W
    rows = jnp.repeat(jnp.arange(n_rows, dtype=F32), GRID_W)
    cols = jnp.tile(jnp.arange(GRID_W, dtype=F32), n_rows)
    d_axis = head_dim // 2
    inv = ROPE_THETA ** (-jnp.arange(0, d_axis, 2, dtype=F32) / d_axis)
    ar = rows[:, None] * inv
    ac = cols[:, None] * inv
    ang = jnp.concatenate([ar, ar, ac, ac], axis=-1)
    ang = jnp.tile(ang, (1, LANES // head_dim))
    chunk = head_dim // 4
    even = (jnp.arange(LANES) // chunk) % 2 == 0
    cos, sin = jnp.cos(ang), jnp.sin(ang)
    sa = jnp.where(even, -sin, 0.0)
    sb = jnp.where(even, 0.0, sin)
    pad = ((0, n_ctx), (0, 0))
    return (jnp.pad(cos, pad, constant_values=1.0), jnp.pad(sa, pad), jnp.pad(sb, pad))


def _rope(y, cos, sa, sb, chunk):
    return y * cos + pltpu.roll(y, LANES - chunk, 1) * sa + pltpu.roll(y, chunk, 1) * sb


def _in_proj_even_kernel(x_ref, g_ref, mod_ref, w_ref, cos_ref, sa_ref, sb_ref, qn_ref, kn_ref,
                         q_ref, k_ref, v_ref, xr_ref, gr_ref):
    h = _modulate(x_ref[...], g_ref[...], mod_ref[0, 0:1, :], mod_ref[0, 1:2, :])
    z = _dot(h.astype(BF16), w_ref[...])
    cos, sa, sb = cos_ref[...], sa_ref[...], sb_ref[...]
    a_q, a_kv, b_w = q_ref.shape[1], k_ref.shape[1], xr_ref.shape[1]

    def norm_rope(zh, gain):
        ms = jnp.mean(zh * zh, axis=-1, keepdims=True)
        return _rope(zh * lax.rsqrt(ms + NORM_EPS) * gain, cos, sa, sb, A_HEAD_DIM // 4)

    scale = A_HEAD_DIM ** -0.5 * LOG2_E
    for hd in range(a_q // LANES):
        cols = slice(hd * LANES, (hd + 1) * LANES)
        q_ref[:, cols] = (norm_rope(z[:, cols], qn_ref[...]) * scale).astype(BF16)
    for hd in range(a_kv // LANES):
        cols = slice(hd * LANES, (hd + 1) * LANES)
        k_ref[:, cols] = norm_rope(z[:, a_q + hd * LANES:a_q + (hd + 1) * LANES], kn_ref[...]).astype(BF16)
    v_ref[...] = z[:, a_q + a_kv:a_q + 2 * a_kv].astype(BF16)
    xr_ref[...] = z[:, a_q + 2 * a_kv:a_q + 2 * a_kv + b_w]
    gr_ref[...] = z[:, a_q + 2 * a_kv + b_w:]


def _row_spec(tm, n):
    return pl.BlockSpec((tm, n), lambda i: (i, 0))


def _full_spec(shape):
    nd = len(shape)
    return pl.BlockSpec(shape, lambda i: (0,) * nd)


def _mod_spec(d, n_lat_tiles):
    return pl.BlockSpec((1, N_MOD, d), lambda i: (jnp.where(i >= n_lat_tiles, 1, 0), 0, 0))


def _in_proj_even(x, g, mod, w, tables, qn, kn, n_lat):
    m, d = x.shape
    tm = ROW_TILE
    a_q = d // 2
    a_kv = a_q // A_GROUP
    b_w = d // 2
    cos, sa, sb = tables
    return pl.pallas_call(
        _in_proj_even_kernel,
        grid=(m // tm,),
        in_specs=[_row_spec(tm, d), _full_spec((1, d)), _mod_spec(d, n_lat // tm), _full_spec(w.shape),
                  _row_spec(tm, LANES), _row_spec(tm, LANES), _row_spec(tm, LANES),
                  _full_spec((1, LANES)), _full_spec((1, LANES))],
        out_specs=[_row_spec(tm, a_q), _row_spec(tm, a_kv), _row_spec(tm, a_kv),
                   _row_spec(tm, b_w), _row_spec(tm, b_w)],
        out_shape=[jax.ShapeDtypeStruct((m, a_q), BF16), jax.ShapeDtypeStruct((m, a_kv), BF16),
                   jax.ShapeDtypeStruct((m, a_kv), BF16), jax.ShapeDtypeStruct((m, b_w), F32),
                   jax.ShapeDtypeStruct((m, b_w), F32)],
        name="in_proj_even",
        compiler_params=_params("parallel"),
    )(x, g.reshape(1, d), mod, w, cos, sa, sb, qn.reshape(1, LANES), kn.reshape(1, LANES))


def _in_proj_odd_kernel(x_ref, g_ref, mod_ref, w_ref, cos_ref, sa_ref, sb_ref,
                        q_ref, qs_ref, k_ref, v_ref):
    h = _modulate(x_ref[...], g_ref[...], mod_ref[0, 0:1, :], mod_ref[0, 1:2, :])
    z = _dot(h.astype(BF16), w_ref[...])
    cos, sa, sb = cos_ref[...], sa_ref[...], sb_ref[...]
    tm = x_ref.shape[0]
    c_q = q_ref.shape[1]
    c_kv = k_ref.shape[1] // 2
    scale = C_HEAD_DIM ** -0.5 * LOG2_E
    for j in range(c_q // LANES):
        cols = slice(j * LANES, (j + 1) * LANES)
        qj = _rope(z[:, cols], cos, sa, sb, C_HEAD_DIM // 4) * scale
        q_ref[:, cols] = qj.astype(BF16)
        qs_ref[:, cols] = pltpu.roll(qj, C_HEAD_DIM, 1).astype(BF16)
    lo = lax.broadcasted_iota(I32, (tm, LANES), 1) < C_HEAD_DIM

    def expand(pair, out_ref, j):
        swapped = pltpu.roll(pair, C_HEAD_DIM, 1)
        out_ref[:, 2 * j * LANES:(2 * j + 1) * LANES] = jnp.where(lo, pair, 0.0).astype(BF16)
        out_ref[:, (2 * j + 1) * LANES:(2 * j + 2) * LANES] = jnp.where(lo, swapped, 0.0).astype(BF16)

    for j in range(c_kv // LANES):
        k0 = c_q + j * LANES
        v0 = c_q + c_kv + j * LANES
        expand(_rope(z[:, k0:k0 + LANES], cos, sa, sb, C_HEAD_DIM // 4), k_ref, j)
        expand(z[:, v0:v0 + LANES], v_ref, j)


def _in_proj_odd(x, g, mod, w, tables, n_lat):
    m, d = x.shape
    tm = ROW_TILE
    c_q = d
    c_kv = d // C_GROUP
    cos, sa, sb = tables
    kv_shape = jax.ShapeDtypeStruct((m, 2 * c_kv), BF16)
    return pl.pallas_call(
        _in_proj_odd_kernel,
        grid=(m // tm,),
        in_specs=[_row_spec(tm, d), _full_spec((1, d)), _mod_spec(d, n_lat // tm), _full_spec(w.shape),
                  _row_spec(tm, LANES), _row_spec(tm, LANES), _row_spec(tm, LANES)],
        out_specs=[_row_spec(tm, c_q)] * 2 + [_row_spec(tm, 2 * c_kv)] * 2,
        out_shape=[jax.ShapeDtypeStruct((m, c_q), BF16)] * 2 + [kv_shape, kv_shape],
        name="in_proj_odd",
        compiler_params=_params("parallel"),
    )(x, g.reshape(1, d), mod, w, cos, sa, sb)


def _dense_attn_kernel(q_ref, k_ref, v_ref, o_ref, m_ref, l_ref, acc_ref, *, n_lat, n_ctx, tk):
    tq = q_ref.shape[0]
    is_lat = pl.program_id(1) < n_lat // tq
    q = jnp.concatenate([q_ref[:, g * LANES:(g + 1) * LANES] for g in range(A_GROUP)], axis=0)
    m_ref[...] = jnp.full(m_ref.shape, NEG_INF, F32)
    l_ref[...] = jnp.zeros(l_ref.shape, F32)
    acc_ref[...] = jnp.zeros(acc_ref.shape, F32)

    def step(kc, vc):
        s = _dot_nt(q, kc)
        m_old = m_ref[...]
        m_new = jnp.maximum(m_old, jnp.max(s, axis=-1, keepdims=True))
        alpha = jnp.exp2(m_old - m_new)
        p = jnp.exp2(s - m_new)
        l_ref[...] = alpha * l_ref[...] + jnp.sum(p, axis=-1, keepdims=True)
        acc_ref[...] = alpha * acc_ref[...] + _dot(p.astype(BF16), vc)
        m_ref[...] = m_new

    @pl.when(is_lat)
    def _():
        def body(i, carry):
            start = pl.multiple_of(i * tk, tk)
            step(k_ref[pl.ds(start, tk), :], v_ref[pl.ds(start, tk), :])
            return carry
        lax.fori_loop(0, n_lat // tk, body, 0, unroll=4)

    step(k_ref[pl.ds(n_lat, n_ctx), :], v_ref[pl.ds(n_lat, n_ctx), :])
    out = acc_ref[...] / l_ref[...]
    for g in range(A_GROUP):
        o_ref[:, g * LANES:(g + 1) * LANES] = out[g * tq:(g + 1) * tq].astype(BF16)


def _dense_attention(q, k, v, n_lat, tk):
    m, a_q = q.shape
    n_kv = k.shape[1] // LANES
    tq = ATT_Q_TILE
    gw = A_GROUP * LANES
    rows = A_GROUP * tq
    return pl.pallas_call(
        functools.partial(_dense_attn_kernel, n_lat=n_lat, n_ctx=m - n_lat, tk=tk),
        grid=(n_kv, m // tq),
        in_specs=[pl.BlockSpec((tq, gw), lambda kh, i: (i, kh)),
                  pl.BlockSpec((m, LANES), lambda kh, i: (0, kh)),
                  pl.BlockSpec((m, LANES), lambda kh, i: (0, kh))],
        out_specs=pl.BlockSpec((tq, gw), lambda kh, i: (i, kh)),
        out_shape=jax.ShapeDtypeStruct((m, a_q), BF16),
        scratch_shapes=[pltpu.VMEM((rows, 1), F32), pltpu.VMEM((rows, 1), F32),
                        pltpu.VMEM((rows, LANES), F32)],
        name="dense_attention",
        compiler_params=_params("parallel", "parallel"),
    )(q, k, v)


def _window_attn_kernel(sink_ref, q_ref, qs_ref, k_ref, v_ref, o_ref, *, n_lat, n_ctx):
    tq = q_ref.shape[0]
    n_pairs = C_GROUP // 2
    n_kv = k_ref.shape[1] // LANES
    span = tq + 2 * WINDOW
    b = pl.program_id(0)
    is_lat = b < n_lat // tq
    ws = pl.multiple_of(jnp.clip(b * tq - WINDOW, 0, n_lat - span), WINDOW)
    off = jnp.where(is_lat, ws - b * tq, 4 * span)
    rel = (lax.broadcasted_iota(I32, (tq, span), 1) - lax.broadcasted_iota(I32, (tq, span), 0)) + off
    valid = jnp.concatenate([jnp.abs(rel) <= WINDOW] * C_GROUP, axis=0)
    rows = n_pairs * tq
    for g in range(n_kv):
        slot = slice(g * LANES, (g + 1) * LANES)
        pairs = [slice((g * n_pairs + j) * LANES, (g * n_pairs + j + 1) * LANES) for j in range(n_pairs)]
        q2 = jnp.concatenate([q_ref[:, p] for p in pairs] + [qs_ref[:, p] for p in pairs], axis=0)
        sk = jnp.concatenate(
            [jnp.full((tq, 1), sink_ref[g * C_GROUP + 2 * j + hi] * LOG2_E, F32)
             for hi in range(2) for j in range(n_pairs)], axis=0)
        s_c = _dot_nt(q2, k_ref[pl.ds(n_lat, n_ctx), slot])
        s_w = jnp.where(valid, _dot_nt(q2, k_ref[pl.ds(ws, span), slot]), NEG_INF)
        mx = jnp.maximum(jnp.maximum(jnp.max(s_c, axis=-1, keepdims=True),
                                     jnp.max(s_w, axis=-1, keepdims=True)), sk)
        e_c = jnp.exp2(s_c - mx)
        e_w = jnp.exp2(s_w - mx)
        den = (jnp.sum(e_c, axis=-1, keepdims=True) + jnp.sum(e_w, axis=-1, keepdims=True)
               + jnp.exp2(sk - mx))
        o2 = (_dot(e_c.astype(BF16), v_ref[pl.ds(n_lat, n_ctx), slot])
              + _dot(e_w.astype(BF16), v_ref[pl.ds(ws, span), slot])) * (1.0 / den)
        out = o2[:rows] + pltpu.roll(o2[rows:], C_HEAD_DIM, 1)
        for j, p in enumerate(pairs):
            o_ref[:, p] = out[j * tq:(j + 1) * tq].astype(BF16)


def _window_attention(q, qs, k, v, sink, n_lat):
    m, c_q = q.shape
    tq = WIN_Q_TILE
    q_spec = pl.BlockSpec((tq, c_q), lambda b, s: (b, 0))
    kv_spec = pl.BlockSpec(k.shape, lambda b, s: (0, 0), pipeline_mode=pl.Buffered(1))
    return pl.pallas_call(
        functools.partial(_window_attn_kernel, n_lat=n_lat, n_ctx=m - n_lat),
        grid_spec=pltpu.PrefetchScalarGridSpec(
            num_scalar_prefetch=1,
            grid=(m // tq,),
            in_specs=[q_spec, q_spec, kv_spec, kv_spec],
            out_specs=q_spec),
        out_shape=jax.ShapeDtypeStruct((m, c_q), BF16),
        name="window_attention",
        compiler_params=_params("parallel"),
    )(sink, q, qs, k, v)


def _rglru_kernel(*refs, reverse, n_tiles, n_lat_tiles):
    if reverse:
        (xp_ref, x_ref, xn_ref, cw_ref, cb_ref, w_ref, rb_ref, ib_ref, lam_ref, hf_ref, gr_ref,
         out_ref, xe_scr, a_scr, u_scr, h_scr, hb_scr) = refs
    else:
        (xp_ref, x_ref, xn_ref, cw_ref, cb_ref, w_ref, rb_ref, ib_ref, lam_ref,
         out_ref, xe_scr, a_scr, u_scr, h_scr) = refs
    tm, bw = x_ref.shape
    pid = pl.program_id(0)
    tile = (n_tiles - 1 - pid) if reverse else (pid + n_lat_tiles) % n_tiles
    seq_start = (tile == 0) | (tile == n_lat_tiles)
    seq_end = (tile == n_lat_tiles - 1) | (tile == n_tiles - 1)

    @pl.when(pid == 0)
    def _():
        h_scr[...] = jnp.zeros(h_scr.shape, F32)

    xe_scr[0:SUBLANES, :] = jnp.where(seq_start, 0.0, xp_ref[...])
    xe_scr[SUBLANES:SUBLANES + tm, :] = x_ref[...]
    xe_scr[SUBLANES + tm:, :] = jnp.where(seq_end, 0.0, xn_ref[...])
    left = CONV_W // 2
    xc = xe_scr[SUBLANES - left:SUBLANES - left + tm, :] * cw_ref[0:1, :]
    for j in range(1, CONV_W):
        s0 = SUBLANES - left + j
        xc = xc + xe_scr[s0:s0 + tm, :] * cw_ref[j:j + 1, :]
    xc = xc + cb_ref[...]

    blk = bw // B_HEADS
    for hd in range(B_HEADS):
        cols = slice(hd * blk, (hd + 1) * blk)
        xh = xc[:, cols]
        zz = _dot(xh.astype(BF16), w_ref[0, hd])
        r = jax.nn.sigmoid(zz[:, :blk] + rb_ref[0, :, cols])
        gi = jax.nn.sigmoid(zz[:, blk:] + ib_ref[0, :, cols])
        lam = lam_ref[0, :, cols]
        log_sig = -(jnp.maximum(-lam, 0.0) + jnp.log1p(jnp.exp(-jnp.abs(lam))))
        log_a = RG_C * r * log_sig
        th = jnp.tanh(log_a)
        a_scr[:, cols] = jnp.exp(log_a)
        u_scr[:, cols] = jnp.sqrt(-2.0 * th / (1.0 - th)) * (gi * xh)

    dst = hb_scr if reverse else out_ref

    def body(j, h):
        t = (tm - 1 - j) if reverse else j
        h = a_scr[pl.ds(t, 1), :] * h + u_scr[pl.ds(t, 1), :]
        dst[pl.ds(t, 1), :] = h
        return h

    h_scr[...] = lax.fori_loop(0, tm, body, h_scr[...], unroll=8)

    if reverse:
        gr = gr_ref[...]
        cdf = 0.5 * (1.0 + jnp.tanh(0.7978845608028654 * (gr + 0.044715 * (gr * gr * gr))))
        out_ref[...] = ((hf_ref[...] + hb_scr[...]) * (gr * cdf)).astype(BF16)


def _rglru(xr, gr, conv_w, conv_b, wcat, rgate_b, igate_b, lam, n_lat):
    m, bw = xr.shape
    tm = ROW_TILE
    n_tiles, n_lat_tiles = m // tm, n_lat // tm
    per_tile = tm // SUBLANES
    n_sub = m // SUBLANES
    blk = bw // B_HEADS

    def run(reverse, extra_in):
        d = 1 if reverse else 0
        if reverse:
            tile = lambda i: n_tiles - 1 - i
        else:
            tile = lambda i: (i + n_lat_tiles) % n_tiles
        row = pl.BlockSpec((tm, bw), lambda i: (tile(i), 0))
        in_specs = [pl.BlockSpec((SUBLANES, bw), lambda i: (jnp.maximum(tile(i) * per_tile - 1, 0), 0)),
                    row,
                    pl.BlockSpec((SUBLANES, bw), lambda i: (jnp.minimum((tile(i) + 1) * per_tile, n_sub - 1), 0)),
                    _full_spec((CONV_W, bw)), _full_spec((1, bw)),
                    pl.BlockSpec((1, B_HEADS, blk, 2 * blk), lambda i: (d, 0, 0, 0)),
                    pl.BlockSpec((1, 1, bw), lambda i: (d, 0, 0)),
                    pl.BlockSpec((1, 1, bw), lambda i: (d, 0, 0)),
                    pl.BlockSpec((1, 1, bw), lambda i: (d, 0, 0))] + [row] * len(extra_in)
        scratch = [pltpu.VMEM((tm + 2 * SUBLANES, bw), F32), pltpu.VMEM((tm, bw), F32),
                   pltpu.VMEM((tm, bw), F32), pltpu.VMEM((1, bw), F32)]
        if reverse:
            scratch.append(pltpu.VMEM((tm, bw), F32))
        return pl.pallas_call(
            functools.partial(_rglru_kernel, reverse=reverse, n_tiles=n_tiles, n_lat_tiles=n_lat_tiles),
            grid=(n_tiles,),
            in_specs=in_specs,
            out_specs=row,
            out_shape=jax.ShapeDtypeStruct((m, bw), BF16 if reverse else F32),
            scratch_shapes=scratch,
            name="rglru_reverse" if reverse else "rglru_forward",
            compiler_params=_params("arbitrary"),
        )(xr, xr, xr, conv_w, conv_b.reshape(1, bw), wcat, rgate_b.reshape(2, 1, bw),
          igate_b.reshape(2, 1, bw), lam.reshape(2, 1, bw), *extra_in)

    hf = run(False, ())
    return run(True, (hf, gr))


def _post_mixer_kernel(*refs, n_att):
    att_refs = refs[:n_att]
    wout_refs = refs[n_att:2 * n_att]
    (x_ref, mod_ref, g2_ref, rwh_ref, rwl_ref, rb_ref,
     x1_ref, h2_ref, eidx_ref, rank_ref, gate_ref, cnt_ref, cnt_scr) = refs[2 * n_att:]
    tm = x_ref.shape[0]

    @pl.when(pl.program_id(0) == 0)
    def _():
        cnt_scr[...] = jnp.zeros(cnt_scr.shape, F32)

    o = _dot(att_refs[0][...], wout_refs[0][...])
    for a_ref, w_ref in zip(att_refs[1:], wout_refs[1:]):
        o = o + _dot(a_ref[...], w_ref[...])
    x1 = x_ref[...] + mod_ref[0, 2:3, :] * o
    x1_ref[...] = x1
    h2 = _modulate(x1, g2_ref[...], mod_ref[0, 3:4, :], mod_ref[0, 4:5, :])
    h2_ref[...] = h2

    hh = h2.astype(BF16)
    hl = (h2 - hh.astype(F32)).astype(BF16)
    both = _dot(hh, rwl_ref[...])
    logits = both[:, :LANES] + _dot(hl, rwh_ref[...]) + both[:, LANES:]
    scores = jax.nn.sigmoid(logits)
    sel = scores + rb_ref[...]
    lane = lax.broadcasted_iota(I32, (tm, LANES), 1)
    picked = jnp.zeros((tm, LANES), jnp.bool_)
    idxs, vals = [], []
    for _ in range(TOP_K):
        mx = jnp.max(sel, axis=-1, keepdims=True)
        idx = jnp.min(jnp.where(sel == mx, lane, LANES), axis=-1, keepdims=True)
        hit = lane == idx
        vals.append(jnp.sum(jnp.where(hit, scores, 0.0), axis=-1, keepdims=True))
        idxs.append(idx)
        sel = jnp.where(hit, -3e38, sel)
        picked = picked | hit
    total = vals[0]
    for v in vals[1:]:
        total = total + v

    pick_f = picked.astype(F32)
    lower = (lax.broadcasted_iota(I32, (tm, tm), 0) > lax.broadcasted_iota(I32, (tm, tm), 1)).astype(BF16)
    rank_dense = _dot(lower, pick_f.astype(BF16)) + cnt_scr[...]
    cnt_scr[...] = cnt_scr[...] + jnp.sum(pick_f, axis=0, keepdims=True)
    cnt_ref[...] = cnt_scr[...]

    eidx = jnp.zeros((tm, LANES), I32)
    rank = jnp.zeros((tm, LANES), I32)
    gate = jnp.zeros((tm, LANES), F32)
    for k in range(TOP_K):
        rk = jnp.sum(jnp.where(lane == idxs[k], rank_dense, 0.0), axis=-1, keepdims=True)
        eidx = jnp.where(lane == k, idxs[k], eidx)
        rank = jnp.where(lane == k, rk.astype(I32), rank)
        gate = jnp.where(lane == k, ROUTED_SCALE * vals[k] / total, gate)
    eidx_ref[...] = eidx
    rank_ref[...] = rank
    gate_ref[...] = gate


def _post_mixer(atts, wouts, x, mod, g2, rwh, rwl, rb, n_lat):
    m, d = x.shape
    tm = ROW_TILE
    n_att = len(atts)
    in_specs = ([_row_spec(tm, a.shape[1]) for a in atts] + [_full_spec(w.shape) for w in wouts]
                + [_row_spec(tm, d), _mod_spec(d, n_lat // tm), _full_spec((1, d)),
                   _full_spec(rwh.shape), _full_spec(rwl.shape), _full_spec((1, LANES))])
    return pl.pallas_call(
        functools.partial(_post_mixer_kernel, n_att=n_att),
        grid=(m // tm,),
        in_specs=in_specs,
        out_specs=[_row_spec(tm, d), _row_spec(tm, d), _row_spec(tm, LANES), _row_spec(tm, LANES),
                   _row_spec(tm, LANES), _full_spec((1, LANES))],
        out_shape=[jax.ShapeDtypeStruct((m, d), F32), jax.ShapeDtypeStruct((m, d), F32),
                   jax.ShapeDtypeStruct((m, LANES), I32), jax.ShapeDtypeStruct((m, LANES), I32),
                   jax.ShapeDtypeStruct((m, LANES), F32), jax.ShapeDtypeStruct((1, LANES), F32)],
        scratch_shapes=[pltpu.VMEM((1, LANES), F32)],
        name="post_mixer",
        compiler_params=_params("arbitrary"),
    )(*atts, *wouts, x, mod, g2.reshape(1, d), rwh, rwl, rb)


def _dispatch_kernel(dest_ref, h2_ref, xs_in, xs_out, sem):
    del xs_in
    tm = h2_ref.shape[0]

    def issue(r, c):
        for k in range(TOP_K):
            pltpu.make_async_copy(h2_ref.at[pl.ds(r, 1)],
                                  xs_out.at[pl.ds(dest_ref[0, 0, r * TOP_K + k], 1)],
                                  sem).start(priority=k % 2)
        return c
    lax.fori_loop(0, tm, issue, 0)
    for _ in range(TOP_K):
        pltpu.make_async_copy(h2_ref, xs_out.at[pl.ds(0, tm)], sem).wait()


def _dispatch(dest, h2, xs):
    m, d = h2.shape
    tm = ROW_TILE
    return pl.pallas_call(
        _dispatch_kernel,
        grid=(m // tm,),
        in_specs=[pl.BlockSpec((1, 1, tm * TOP_K), lambda i: (i, 0, 0), memory_space=pltpu.SMEM),
                  _row_spec(tm, d), pl.BlockSpec(memory_space=pl.ANY)],
        out_specs=pl.BlockSpec(memory_space=pl.ANY),
        out_shape=jax.ShapeDtypeStruct(xs.shape, xs.dtype),
        scratch_shapes=[pltpu.SemaphoreType.DMA],
        input_output_aliases={2: 0},
        name="dispatch",
        compiler_params=_params("arbitrary", disable_bounds_checks=True),
    )(dest.reshape(m // tm, 1, tm * TOP_K), h2, xs)


def _expert_kernel(ps_ref, cnt_ref, xs_hbm, wg_ref, wu_ref, wd_ref, y_hbm,
                   xbuf, ybuf, wgb, wub, wdb, in_sem, out_sem):
    del xs_hbm
    bm = xbuf.shape[1]
    e = pl.program_id(0)
    n_e = pl.num_programs(0)
    base = ps_ref[e]
    nblk = (cnt_ref[e] + bm - 1) // bm
    tail_rows = cnt_ref[e] - (nblk - 1) * bm

    def rows(j):
        return pl.ds(pl.multiple_of(base + j * bm, bm), bm)

    def in_copy(j, slot):
        return pltpu.make_async_copy(y_hbm.at[rows(j)], xbuf.at[slot], in_sem.at[slot])

    def out_copy(j, slot):
        return pltpu.make_async_copy(ybuf.at[slot], y_hbm.at[rows(j)], out_sem.at[slot])

    def first_copy(ex):
        start = pl.multiple_of(ps_ref[ex], bm)
        return pltpu.make_async_copy(y_hbm.at[pl.ds(start, bm)], xbuf.at[0], in_sem.at[0])

    def drain(n):
        for back in (1, 2):
            @pl.when(n >= back)
            def _():
                out_copy(0, (n - back) % 2).wait()

    @pl.when((e == 0) & (nblk > 0))
    def _():
        first_copy(0).start(priority=BLOCK_DMA_QUEUE)

    @pl.when(e == 0)
    def _():
        ybuf[...] = jnp.zeros(ybuf.shape, F32)

    wgb[...] = wg_ref[0, 0].astype(BF16)
    wub[...] = wu_ref[0, 0].astype(BF16)
    wdb[...] = wd_ref[0, 0].astype(BF16)

    @pl.when(e > 0)
    def _():
        drain((cnt_ref[jnp.maximum(e - 1, 0)] + bm - 1) // bm)

    def block(j, slot):
        in_copy(j, slot).wait()

        @pl.when(j + 1 < nblk)
        def _():
            in_copy(j + 1, 1 - slot).start(priority=BLOCK_DMA_QUEUE)

        @pl.when(j >= 2)
        def _():
            out_copy(j - 2, slot).wait()

        def compute(n):
            xb = xbuf[slot, :n].astype(BF16)
            hm = _silu(_dot(xb, wgb[...])) * _dot(xb, wub[...])
            ybuf[slot, :n] = _dot(hm.astype(BF16), wdb[...])

        short = (j == nblk - 1) & (tail_rows <= bm // 2)

        @pl.when(short)
        def _():
            compute(bm // 2)

        @pl.when(jnp.logical_not(short))
        def _():
            compute(bm)

        out_copy(j, slot).start(priority=BLOCK_DMA_QUEUE)

    def pair(jj, c):
        block(2 * jj, 0)

        @pl.when(2 * jj + 1 < nblk)
        def _():
            block(2 * jj + 1, 1)
        return c
    lax.fori_loop(0, (nblk + 1) // 2, pair, 0)

    nxt = jnp.minimum(e + 1, n_e - 1)

    @pl.when((e + 1 < n_e) & (cnt_ref[nxt] > 0))
    def _():
        first_copy(nxt).start(priority=BLOCK_DMA_QUEUE)

    @pl.when(e == n_e - 1)
    def _():
        drain(nblk)


def _experts(xs, pad_start, counts, layer, w_gate, w_up, w_down):
    _, n_e, d, d_e = w_gate.shape
    bm = EXPERT_ROWS
    return pl.pallas_call(
        _expert_kernel,
        grid_spec=pltpu.PrefetchScalarGridSpec(
            num_scalar_prefetch=2,
            grid=(n_e,),
            in_specs=[pl.BlockSpec(memory_space=pl.ANY),
                      pl.BlockSpec((1, 1, d, d_e), lambda e, ps, cn: (layer, e, 0, 0)),
                      pl.BlockSpec((1, 1, d, d_e), lambda e, ps, cn: (layer, e, 0, 0)),
                      pl.BlockSpec((1, 1, d_e, d), lambda e, ps, cn: (layer, e, 0, 0))],
            out_specs=pl.BlockSpec(memory_space=pl.ANY),
            scratch_shapes=[pltpu.VMEM((2, bm, d), F32), pltpu.VMEM((2, bm, d), F32),
                            pltpu.VMEM((d, d_e), BF16), pltpu.VMEM((d, d_e), BF16),
                            pltpu.VMEM((d_e, d), BF16),
                            pltpu.SemaphoreType.DMA((2,)), pltpu.SemaphoreType.DMA((2,))]),
        out_shape=jax.ShapeDtypeStruct(xs.shape, xs.dtype),
        input_output_aliases={2: 0},
        name="experts",
        compiler_params=_params("arbitrary"),
    )(pad_start, counts, xs, w_gate, w_up, w_down)


def _combine_kernel(*refs, final, n_steps):
    if final:
        (dest_ref, next_dest_ref, y_hbm, x1_ref, h2_ref, gate_ref, mod_ref, sg_ref, su_ref, sd_ref,
         fg_ref, o_ref, ybuf, routed_scr, sem) = refs
    else:
        (dest_ref, next_dest_ref, y_hbm, x1_ref, h2_ref, gate_ref, mod_ref, sg_ref, su_ref, sd_ref,
         o_ref, ybuf, routed_scr, sem) = refs
    tc = x1_ref.shape[0]
    i = pl.program_id(0)
    slot = i % 2

    def issue_row(d_ref, r, s):
        for k in range(TOP_K):
            pltpu.make_async_copy(y_hbm.at[pl.ds(d_ref[0, 0, r * TOP_K + k], 1)],
                                  ybuf.at[s, k, pl.ds(r, 1)], sem.at[s]).start(priority=k % 2)

    def wait_tile(s):
        for k in range(TOP_K):
            pltpu.make_async_copy(y_hbm.at[pl.ds(0, tc)], ybuf.at[s, k], sem.at[s]).wait()

    @pl.when(i == 0)
    def _():
        def first(r, c):
            issue_row(dest_ref, r, 0)
            return c
        lax.fori_loop(0, tc, first, 0)

    wait_tile(slot)

    def body(g, c):
        r0 = pl.multiple_of(g * SUBLANES, SUBLANES)
        for rr in range(SUBLANES):
            issue_row(next_dest_ref, r0 + rr, 1 - slot)
        gate = gate_ref[pl.ds(r0, SUBLANES), :]
        acc = gate[:, 0:1] * ybuf[slot, 0, pl.ds(r0, SUBLANES), :]
        for k in range(1, TOP_K):
            acc = acc + gate[:, k:k + 1] * ybuf[slot, k, pl.ds(r0, SUBLANES), :]
        routed_scr[pl.ds(r0, SUBLANES), :] = acc
        return c
    lax.fori_loop(0, tc // SUBLANES, body, 0)

    @pl.when(i == n_steps - 1)
    def _():
        wait_tile(1 - slot)

    hb = h2_ref[...].astype(BF16)
    shared = _dot((_silu(_dot(hb, sg_ref[...])) * _dot(hb, su_ref[...])).astype(BF16), sd_ref[...])
    x2 = x1_ref[...] + mod_ref[0, 5:6, :] * (routed_scr[...] + shared)
    if final:
        ms = jnp.mean(x2 * x2, axis=-1, keepdims=True)
        x2 = x2 * lax.rsqrt(ms + NORM_EPS) * fg_ref[...]
    o_ref[...] = x2


def _combine(dest, y, x1, h2, gate, mod, sg, su, sd, final_g, n_lat):
    m, d = x1.shape
    tc = COMBINE_TILE
    n_steps = m // tc
    final = final_g is not None
    dest3 = dest.reshape(n_steps, 1, tc * TOP_K)
    in_specs = [pl.BlockSpec((1, 1, tc * TOP_K), lambda i: (i, 0, 0), memory_space=pltpu.SMEM),
                pl.BlockSpec((1, 1, tc * TOP_K), lambda i: (jnp.minimum(i + 1, n_steps - 1), 0, 0),
                             memory_space=pltpu.SMEM),
                pl.BlockSpec(memory_space=pl.ANY),
                _row_spec(tc, d), _row_spec(tc, d), _row_spec(tc, LANES), _mod_spec(d, n_lat // tc),
                _full_spec(sg.shape), _full_spec(su.shape), _full_spec(sd.shape)]
    args = [dest3, dest3, y, x1, h2, gate, mod, sg, su, sd]
    if final:
        in_specs.append(_full_spec((1, d)))
        args.append(final_g.reshape(1, d))
    return pl.pallas_call(
        functools.partial(_combine_kernel, final=final, n_steps=n_steps),
        grid=(n_steps,),
        in_specs=in_specs,
        out_specs=_row_spec(tc, d),
        out_shape=jax.ShapeDtypeStruct((m, d), F32),
        scratch_shapes=[pltpu.VMEM((2, TOP_K, tc, d), F32), pltpu.VMEM((tc, d), F32),
                        pltpu.SemaphoreType.DMA((2,))],
        name="combine",
        compiler_params=_params("arbitrary", disable_bounds_checks=True),
    )(*args)


def _sorted_rows(m, n_experts, bm):
    return -(-(m * TOP_K + n_experts * (bm - 1)) // bm) * bm


def _routing_tables(eidx, rank, counts, bm):
    padded = (counts + bm - 1) // bm * bm
    pad_start = jnp.cumsum(padded) - padded
    experts = jnp.arange(counts.shape[0], dtype=I32)
    start_of_pick = jnp.sum(jnp.where(eidx[:, :, None] == experts, pad_start, 0), axis=-1)
    return pad_start.astype(I32), (start_of_pick + rank).astype(I32)


def kernel(x, c, ctx, c_ctx, w_mod, b_mod, norm1_g, norm2_g, final_norm_g, a_w_in, a_w_out, a_q_norm,
           a_k_norm, b_conv_w, b_conv_b, b_rgate_w, b_rgate_b, b_igate_w, b_igate_b, b_lambda, c_w_in,
           c_w_out, c_sink, moe_router_w, moe_router_b, moe_w_gate, moe_w_up, moe_w_down,
           moe_shared_gate, moe_shared_up, moe_shared_down):
    batch, n_lat, d = x.shape
    n_ctx = ctx.shape[1]
    depth = w_mod.shape[0]
    n_experts = moe_router_w.shape[2]
    assert batch == 1 and n_ctx == ROW_TILE and n_lat % ROW_TILE == 0
    assert n_lat >= WIN_Q_TILE + 2 * WINDOW and n_experts <= LANES

    tk = min(ATT_KV_CHUNK, n_lat)
    xs = jnp.concatenate([x[0], ctx[0]], axis=0)
    mods = _ada_params(jnp.stack([c[0], c_ctx]), w_mod, b_mod).reshape(depth, 2, N_MOD, d)
    tables_a = _rope_tables(n_lat, n_ctx, A_HEAD_DIM)
    tables_c = _rope_tables(n_lat, n_ctx, C_HEAD_DIM)

    pad_e = LANES - n_experts
    rw = jnp.pad(moe_router_w, ((0, 0), (0, 0), (0, pad_e)))
    rwh = rw.astype(BF16)
    rwl = jnp.concatenate([rwh, (rw - rwh.astype(F32)).astype(BF16)], axis=-1)
    rb = jnp.pad(moe_router_b, ((0, 0), (0, pad_e)), constant_values=NEG_INF).reshape(depth, 1, LANES)
    sorted_x = jnp.zeros((_sorted_rows(n_lat + n_ctx, n_experts, EXPERT_ROWS), d), F32)

    for layer in range(depth):
        i = layer // 2
        mod = mods[layer]
        if layer % 2 == 0:
            q, k, v, xr, gr = _in_proj_even(xs, norm1_g[layer], mod, a_w_in[i].astype(BF16), tables_a,
                                            a_q_norm[i], a_k_norm[i], n_lat)
            att = _dense_attention(q, k, v, n_lat, tk)
            wcat = jnp.concatenate([b_rgate_w[i], b_igate_w[i]], axis=-1).astype(BF16)
            rec = _rglru(xr, gr, b_conv_w[i], b_conv_b[i], wcat, b_rgate_b[i], b_igate_b[i],
                         b_lambda[i], n_lat)
            w_out = a_w_out[i].astype(BF16)
            a_q = att.shape[1]
            atts, wouts = (att, rec), (w_out[:a_q], w_out[a_q:])
        else:
            q, qs, k, v = _in_proj_odd(xs, norm1_g[layer], mod, c_w_in[i].astype(BF16), tables_c, n_lat)
            att = _window_attention(q, qs, k, v, c_sink[i], n_lat)
            atts, wouts = (att,), (c_w_out[i].astype(BF16),)
        x1, h2, eidx, rank, gate, cnt = _post_mixer(atts, wouts, xs, mod, norm2_g[layer],
                                                    rwh[layer], rwl[layer], rb[layer], n_lat)
        counts = cnt[0, :n_experts].astype(I32)
        pad_start, dest = _routing_tables(eidx[:, :TOP_K], rank[:, :TOP_K], counts, EXPERT_ROWS)
        sorted_x = _dispatch(dest, h2, sorted_x)
        sorted_x = _experts(sorted_x, pad_start, counts, layer, moe_w_gate, moe_w_up, moe_w_down)
        xs = _combine(dest, sorted_x, x1, h2, gate, mod, moe_shared_gate[layer].astype(BF16),
                      moe_shared_up[layer].astype(BF16), moe_shared_down[layer].astype(BF16),
                      final_norm_g if layer == depth - 1 else None, n_lat)
    return xs[:n_lat].reshape(batch, n_lat, d)
```

```python
import functools

import jax
import jax.numpy as jnp
from jax import lax
from jax.experimental import pallas as pl
from jax.experimental.pallas import tpu as pltpu

F32 = jnp.float32
BF16 = jnp.bfloat16
I32 = jnp.int32

NORM_EPS = 1e-6
ROPE_THETA = 10000.0
GRID_W = 64
N_MOD = 6
A_HEAD_DIM = 128
A_GROUP = 4
B_HEADS = 8
CONV_W = 4
RG_C = 8.0
C_HEAD_DIM = 64
C_GROUP = 8
WINDOW = 128
TOP_K = 8
ROUTED_SCALE = 2.5
NEG_INF = -1e30
LOG2_E = 1.4426950408889634

LANES = 128
SUBLANES = 8
ROW_TILE = 256
ATT_Q_TILE = 256
ATT_KV_CHUNK = 2048
WIN_Q_TILE = 128
EXPERT_ROWS = 384
COMBINE_TILE = 128
ADA_COLS = 512
BLOCK_DMA_QUEUE = 0
VMEM_LIMIT = 48 * 1024 * 1024


def _params(*sem, **kw):
    return pltpu.CompilerParams(dimension_semantics=sem, vmem_limit_bytes=VMEM_LIMIT, **kw)


def _modulate(xf, g, shift, scale):
    ms = jnp.mean(xf * xf, axis=-1, keepdims=True)
    y = xf * lax.rsqrt(ms + NORM_EPS) * g
    return y * (1.0 + scale) + shift


def _silu(x):
    return x * jax.nn.sigmoid(x)


def _dot(a, b):
    return jnp.dot(a, b, preferred_element_type=F32)


def _dot_nt(a, b):
    return lax.dot_general(a, b, (((1,), (1,)), ((), ())), preferred_element_type=F32)


def _ada_kernel(cond_ref, w_ref, b_ref, o_ref):
    tn = w_ref.shape[2]
    for cnd in range(2):
        s = _silu(cond_ref[cnd])
        for j in range(tn // LANES):
            cols = slice(j * LANES, (j + 1) * LANES)
            acc = jnp.sum(w_ref[0, :, cols] * s, axis=0, keepdims=True)
            o_ref[0, cnd:cnd + 1, cols] = acc + b_ref[0, :, cols]


def _ada_params(cond, w_mod, b_mod):
    depth, d, n = w_mod.shape
    cond_b = jnp.broadcast_to(cond[:, :, None], (2, d, LANES))
    return pl.pallas_call(
        _ada_kernel,
        grid=(depth, n // ADA_COLS),
        in_specs=[pl.BlockSpec((2, d, LANES), lambda l, j: (0, 0, 0)),
                  pl.BlockSpec((1, d, ADA_COLS), lambda l, j: (l, 0, j)),
                  pl.BlockSpec((1, 1, ADA_COLS), lambda l, j: (l, 0, j))],
        out_specs=pl.BlockSpec((1, 2, ADA_COLS), lambda l, j: (l, 0, j)),
        out_shape=jax.ShapeDtypeStruct((depth, 2, n), F32),
        name="ada_params",
        compiler_params=_params("parallel", "parallel"),
    )(cond_b, w_mod, b_mod.reshape(depth, 1, n))


def _rope_tables(n_lat, n_ctx, head_dim):
    n_rows = n_lat // GRID_W
    rows = jnp.repeat(jnp.arange(n_rows, dtype=F32), GRID_W)
    cols = jnp.tile(jnp.arange(GRID_W, dtype=F32), n_rows)
    d_axis = head_dim // 2
    inv = ROPE_THETA ** (-jnp.arange(0, d_axis, 2, dtype=F32) / d_axis)
    ar = rows[:, None] * inv
    ac = cols[:, None] * inv
    ang = jnp.concatenate([ar, ar, ac, ac], axis=-1)
    ang = jnp.tile(ang, (1, LANES // head_dim))
    chunk = head_dim // 4
    even = (jnp.arange(LANES) // chunk) % 2 == 0
    cos, sin = jnp.cos(ang), jnp.sin(ang)
    sa = jnp.where(even, -sin, 0.0)
    sb = jnp.where(even, 0.0, sin)
    pad = ((0, n_ctx), (0, 0))
    return (jnp.pad(cos, pad, constant_values=1.0), jnp.pad(sa, pad), jnp.pad(sb, pad))


def _rope(y, cos, sa, sb, chunk):
    return y * cos + pltpu.roll(y, LANES - chunk, 1) * sa + pltpu.roll(y, chunk, 1) * sb


def _in_proj_even_kernel(x_ref, g_ref, mod_ref, w_ref, cos_ref, sa_ref, sb_ref, qn_ref, kn_ref,
                         q_ref, k_ref, v_ref, xr_ref, gr_ref):
    h = _modulate(x_ref[...], g_ref[...], mod_ref[0, 0:1, :], mod_ref[0, 1:2, :])
    z = _dot(h.astype(BF16), w_ref[...])
    cos, sa, sb = cos_ref[...], sa_ref[...], sb_ref[...]
    a_q, a_kv, b_w = q_ref.shape[1], k_ref.shape[1], xr_ref.shape[1]

    def norm_rope(zh, gain):
        ms = jnp.mean(zh * zh, axis=-1, keepdims=True)
        return _rope(zh * lax.rsqrt(ms + NORM_EPS) * gain, cos, sa, sb, A_HEAD_DIM // 4)

    scale = A_HEAD_DIM ** -0.5 * LOG2_E
    for hd in range(a_q // LANES):
        cols = slice(hd * LANES, (hd + 1) * LANES)
        q_ref[:, cols] = (norm_rope(z[:, cols], qn_ref[...]) * scale).astype(BF16)
    for hd in range(a_kv // LANES):
        cols = slice(hd * LANES, (hd + 1) * LANES)
        k_ref[:, cols] = norm_rope(z[:, a_q + hd * LANES:a_q + (hd + 1) * LANES], kn_ref[...]).astype(BF16)
    v_ref[...] = z[:, a_q + a_kv:a_q + 2 * a_kv].astype(BF16)
    xr_ref[...] = z[:, a_q + 2 * a_kv:a_q + 2 * a_kv + b_w]
    gr_ref[...] = z[:, a_q + 2 * a_kv + b_w:]


def _row_spec(tm, n):
    return pl.BlockSpec((tm, n), lambda i: (i, 0))


def _full_spec(shape):
    nd = len(shape)
    return pl.BlockSpec(shape, lambda i: (0,) * nd)


def _mod_spec(d, n_lat_tiles):
    return pl.BlockSpec((1, N_MOD, d), lambda i: (jnp.where(i >= n_lat_tiles, 1, 0), 0, 0))


def _in_proj_even(x, g, mod, w, tables, qn, kn, n_lat):
    m, d = x.shape
    tm = ROW_TILE
    a_q = d // 2
    a_kv = a_q // A_GROUP
    b_w = d // 2
    cos, sa, sb = tables
    return pl.pallas_call(
        _in_proj_even_kernel,
        grid=(m // tm,),
        in_specs=[_row_spec(tm, d), _full_spec((1, d)), _mod_spec(d, n_lat // tm), _full_spec(w.shape),
                  _row_spec(tm, LANES), _row_spec(tm, LANES), _row_spec(tm, LANES),
                  _full_spec((1, LANES)), _full_spec((1, LANES))],
        out_specs=[_row_spec(tm, a_q), _row_spec(tm, a_kv), _row_spec(tm, a_kv),
                   _row_spec(tm, b_w), _row_spec(tm, b_w)],
        out_shape=[jax.ShapeDtypeStruct((m, a_q), BF16), jax.ShapeDtypeStruct((m, a_kv), BF16),
                   jax.ShapeDtypeStruct((m, a_kv), BF16), jax.ShapeDtypeStruct((m, b_w), F32),
                   jax.ShapeDtypeStruct((m, b_w), F32)],
        name="in_proj_even",
        compiler_params=_params("parallel"),
    )(x, g.reshape(1, d), mod, w, cos, sa, sb, qn.reshape(1, LANES), kn.reshape(1, LANES))


def _in_proj_odd_kernel(x_ref, g_ref, mod_ref, w_ref, cos_ref, sa_ref, sb_ref,
                        q_ref, qs_ref, k_ref, v_ref):
    h = _modulate(x_ref[...], g_ref[...], mod_ref[0, 0:1, :], mod_ref[0, 1:2, :])
    z = _dot(h.astype(BF16), w_ref[...])
    cos, sa, sb = cos_ref[...], sa_ref[...], sb_ref[...]
    tm = x_ref.shape[0]
    c_q = q_ref.shape[1]
    c_kv = k_ref.shape[1] // 2
    scale = C_HEAD_DIM ** -0.5 * LOG2_E
    for j in range(c_q // LANES):
        cols = slice(j * LANES, (j + 1) * LANES)
        qj = _rope(z[:, cols], cos, sa, sb, C_HEAD_DIM // 4) * scale
        q_ref[:, cols] = qj.astype(BF16)
        qs_ref[:, cols] = pltpu.roll(qj, C_HEAD_DIM, 1).astype(BF16)
    lo = lax.broadcasted_iota(I32, (tm, LANES), 1) < C_HEAD_DIM

    def expand(pair, out_ref, j):
        swapped = pltpu.roll(pair, C_HEAD_DIM, 1)
        out_ref[:, 2 * j * LANES:(2 * j + 1) * LANES] = jnp.where(lo, pair, 0.0).astype(BF16)
        out_ref[:, (2 * j + 1) * LANES:(2 * j + 2) * LANES] = jnp.where(lo, swapped, 0.0).astype(BF16)

    for j in range(c_kv // LANES):
        k0 = c_q + j * LANES
        v0 = c_q + c_kv + j * LANES
        expand(_rope(z[:, k0:k0 + LANES], cos, sa, sb, C_HEAD_DIM // 4), k_ref, j)
        expand(z[:, v0:v0 + LANES], v_ref, j)


def _in_proj_odd(x, g, mod, w, tables, n_lat):
    m, d = x.shape
    tm = ROW_TILE
    c_q = d
    c_kv = d // C_GROUP
    cos, sa, sb = tables
    kv_shape = jax.ShapeDtypeStruct((m, 2 * c_kv), BF16)
    return pl.pallas_call(
        _in_proj_odd_kernel,
        grid=(m // tm,),
        in_specs=[_row_spec(tm, d), _full_spec((1, d)), _mod_spec(d, n_lat // tm), _full_spec(w.shape),
                  _row_spec(tm, LANES), _row_spec(tm, LANES), _row_spec(tm, LANES)],
        out_specs=[_row_spec(tm, c_q)] * 2 + [_row_spec(tm, 2 * c_kv)] * 2,
        out_shape=[jax.ShapeDtypeStruct((m, c_q), BF16)] * 2 + [kv_shape, kv_shape],
        name="in_proj_odd",
        compiler_params=_params("parallel"),
    )(x, g.reshape(1, d), mod, w, cos, sa, sb)


def _dense_attn_kernel(q_ref, k_ref, v_ref, o_ref, m_ref, l_ref, acc_ref, *, n_lat, n_ctx, tk):
    tq = q_ref.shape[0]
    is_lat = pl.program_id(1) < n_lat // tq
    q = jnp.concatenate([q_ref[:, g * LANES:(g + 1) * LANES] for g in range(A_GROUP)], axis=0)
    s0 = _dot_nt(q, k_ref[pl.ds(n_lat, n_ctx), :])
    m0 = jnp.max(s0, axis=-1, keepdims=True)
    p0 = jnp.exp2(s0 - m0)
    m_ref[...] = m0
    l_ref[...] = jnp.sum(p0, axis=-1, keepdims=True)
    acc_ref[...] = _dot(p0.astype(BF16), v_ref[pl.ds(n_lat, n_ctx), :])

    def step(kc, vc):
        s = _dot_nt(q, kc)
        m_old = m_ref[...]
        m_new = jnp.maximum(m_old, jnp.max(s, axis=-1, keepdims=True))
        alpha = jnp.exp2(m_old - m_new)
        p = jnp.exp2(s - m_new)
        l_ref[...] = alpha * l_ref[...] + jnp.sum(p, axis=-1, keepdims=True)
        acc_ref[...] = alpha * acc_ref[...] + _dot(p.astype(BF16), vc)
        m_ref[...] = m_new

    @pl.when(is_lat)
    def _():
        def body(i, carry):
            start = pl.multiple_of(i * tk, tk)
            step(k_ref[pl.ds(start, tk), :], v_ref[pl.ds(start, tk), :])
            return carry
        lax.fori_loop(0, n_lat // tk, body, 0, unroll=4)

    out = acc_ref[...] / l_ref[...]
    for g in range(A_GROUP):
        o_ref[:, g * LANES:(g + 1) * LANES] = out[g * tq:(g + 1) * tq].astype(BF16)


def _dense_attention(q, k, v, n_lat, tk):
    m, a_q = q.shape
    n_kv = k.shape[1] // LANES
    tq = ATT_Q_TILE
    gw = A_GROUP * LANES
    rows = A_GROUP * tq
    return pl.pallas_call(
        functools.partial(_dense_attn_kernel, n_lat=n_lat, n_ctx=m - n_lat, tk=tk),
        grid=(n_kv, m // tq),
        in_specs=[pl.BlockSpec((tq, gw), lambda kh, i: (i, kh)),
                  pl.BlockSpec((m, LANES), lambda kh, i: (0, kh)),
                  pl.BlockSpec((m, LANES), lambda kh, i: (0, kh))],
        out_specs=pl.BlockSpec((tq, gw), lambda kh, i: (i, kh)),
        out_shape=jax.ShapeDtypeStruct((m, a_q), BF16),
        scratch_shapes=[pltpu.VMEM((rows, 1), F32), pltpu.VMEM((rows, 1), F32),
                        pltpu.VMEM((rows, LANES), F32)],
        name="dense_attention",
        compiler_params=_params("parallel", "parallel"),
    )(q, k, v)


def _window_attn_kernel(sink_ref, q_ref, qs_ref, k_ref, v_ref, o_ref, *, n_lat, n_ctx):
    tq = q_ref.shape[0]
    n_pairs = C_GROUP // 2
    n_kv = k_ref.shape[1] // LANES
    span = tq + 2 * WINDOW
    b = pl.program_id(0)
    is_lat = b < n_lat // tq
    ws = pl.multiple_of(jnp.clip(b * tq - WINDOW, 0, n_lat - span), WINDOW)
    off = jnp.where(is_lat, ws - b * tq, 4 * span)
    rel = (lax.broadcasted_iota(I32, (tq, span), 1) - lax.broadcasted_iota(I32, (tq, span), 0)) + off
    valid = jnp.concatenate([jnp.abs(rel) <= WINDOW] * C_GROUP, axis=0)
    rows = n_pairs * tq
    for g in range(n_kv):
        slot = slice(g * LANES, (g + 1) * LANES)
        pairs = [slice((g * n_pairs + j) * LANES, (g * n_pairs + j + 1) * LANES) for j in range(n_pairs)]
        q2 = jnp.concatenate([q_ref[:, p] for p in pairs] + [qs_ref[:, p] for p in pairs], axis=0)
        sk = jnp.concatenate(
            [jnp.full((tq, 1), sink_ref[g * C_GROUP + 2 * j + hi] * LOG2_E, F32)
             for hi in range(2) for j in range(n_pairs)], axis=0)
        s_c = _dot_nt(q2, k_ref[pl.ds(n_lat, n_ctx), slot])
        s_w = jnp.where(valid, _dot_nt(q2, k_ref[pl.ds(ws, span), slot]), NEG_INF)
        mx = jnp.maximum(jnp.maximum(jnp.max(s_c, axis=-1, keepdims=True),
                                     jnp.max(s_w, axis=-1, keepdims=True)), sk)
        e_c = jnp.exp2(s_c - mx)
        e_w = jnp.exp2(s_w - mx)
        den = (jnp.sum(e_c, axis=-1, keepdims=True) + jnp.sum(e_w, axis=-1, keepdims=True)
               + jnp.exp2(sk - mx))
        o2 = (_dot(e_c.astype(BF16), v_ref[pl.ds(n_lat, n_ctx), slot])
              + _dot(e_w.astype(BF16), v_ref[pl.ds(ws, span), slot])) * (1.0 / den)
        out = o2[:rows] + pltpu.roll(o2[rows:], C_HEAD_DIM, 1)
        for j, p in enumerate(pairs):
            o_ref[:, p] = out[j * tq:(j + 1) * tq].astype(BF16)


def _window_attention(q, qs, k, v, sink, n_lat):
    m, c_q = q.shape
    tq = WIN_Q_TILE
    q_spec = pl.BlockSpec((tq, c_q), lambda b, s: (b, 0))
    kv_spec = pl.BlockSpec(k.shape, lambda b, s: (0, 0), pipeline_mode=pl.Buffered(1))
    return pl.pallas_call(
        functools.partial(_window_attn_kernel, n_lat=n_lat, n_ctx=m - n_lat),
        grid_spec=pltpu.PrefetchScalarGridSpec(
            num_scalar_prefetch=1,
            grid=(m // tq,),
            in_specs=[q_spec, q_spec, kv_spec, kv_spec],
            out_specs=q_spec),
        out_shape=jax.ShapeDtypeStruct((m, c_q), BF16),
        name="window_attention",
        compiler_params=_params("parallel"),
    )(sink, q, qs, k, v)


def _rglru_kernel(*refs, reverse, n_tiles, n_lat_tiles):
    if reverse:
        (xp_ref, x_ref, xn_ref, cw_ref, cb_ref, w_ref, rb_ref, ib_ref, lam_ref, hf_ref, gr_ref,
         out_ref, xe_scr, a_scr, u_scr, h_scr, hb_scr) = refs
    else:
        (xp_ref, x_ref, xn_ref, cw_ref, cb_ref, w_ref, rb_ref, ib_ref, lam_ref,
         out_ref, xe_scr, a_scr, u_scr, h_scr) = refs
    tm, bw = x_ref.shape
    pid = pl.program_id(0)
    tile = (n_tiles - 1 - pid) if reverse else (pid + n_lat_tiles) % n_tiles
    seq_start = (tile == 0) | (tile == n_lat_tiles)
    seq_end = (tile == n_lat_tiles - 1) | (tile == n_tiles - 1)

    @pl.when(pid == 0)
    def _():
        h_scr[...] = jnp.zeros(h_scr.shape, F32)

    xe_scr[0:SUBLANES, :] = jnp.where(seq_start, 0.0, xp_ref[...])
    xe_scr[SUBLANES:SUBLANES + tm, :] = x_ref[...]
    xe_scr[SUBLANES + tm:, :] = jnp.where(seq_end, 0.0, xn_ref[...])
    left = CONV_W // 2
    xc = xe_scr[SUBLANES - left:SUBLANES - left + tm, :] * cw_ref[0:1, :]
    for j in range(1, CONV_W):
        s0 = SUBLANES - left + j
        xc = xc + xe_scr[s0:s0 + tm, :] * cw_ref[j:j + 1, :]
    xc = xc + cb_ref[...]

    blk = bw // B_HEADS
    for hd in range(B_HEADS):
        cols = slice(hd * blk, (hd + 1) * blk)
        xh = xc[:, cols]
        zz = _dot(xh.astype(BF16), w_ref[0, hd])
        r = jax.nn.sigmoid(zz[:, :blk] + rb_ref[0, :, cols])
        gi = jax.nn.sigmoid(zz[:, blk:] + ib_ref[0, :, cols])
        lam = lam_ref[0, :, cols]
        log_sig = -(jnp.maximum(-lam, 0.0) + jnp.log1p(jnp.exp(-jnp.abs(lam))))
        log_a = RG_C * r * log_sig
        th = jnp.tanh(log_a)
        a_scr[:, cols] = jnp.exp(log_a)
        u_scr[:, cols] = jnp.sqrt(-2.0 * th / (1.0 - th)) * (gi * xh)

    dst = hb_scr if reverse else out_ref

    def body(j, h):
        t = (tm - 1 - j) if reverse else j
        h = a_scr[pl.ds(t, 1), :] * h + u_scr[pl.ds(t, 1), :]
        dst[pl.ds(t, 1), :] = h
        return h

    h_scr[...] = lax.fori_loop(0, tm, body, h_scr[...], unroll=8)

    if reverse:
        gr = gr_ref[...]
        cdf = 0.5 * (1.0 + jnp.tanh(0.7978845608028654 * (gr + 0.044715 * (gr * gr * gr))))
        out_ref[...] = ((hf_ref[...] + hb_scr[...]) * (gr * cdf)).astype(BF16)


def _rglru(xr, gr, conv_w, conv_b, wcat, rgate_b, igate_b, lam, n_lat):
    m, bw = xr.shape
    tm = ROW_TILE
    n_tiles, n_lat_tiles = m // tm, n_lat // tm
    per_tile = tm // SUBLANES
    n_sub = m // SUBLANES
    blk = bw // B_HEADS

    def run(reverse, extra_in):
        d = 1 if reverse else 0
        if reverse:
            tile = lambda i: n_tiles - 1 - i
        else:
            tile = lambda i: (i + n_lat_tiles) % n_tiles
        row = pl.BlockSpec((tm, bw), lambda i: (tile(i), 0))
        in_specs = [pl.BlockSpec((SUBLANES, bw), lambda i: (jnp.maximum(tile(i) * per_tile - 1, 0), 0)),
                    row,
                    pl.BlockSpec((SUBLANES, bw), lambda i: (jnp.minimum((tile(i) + 1) * per_tile, n_sub - 1), 0)),
                    _full_spec((CONV_W, bw)), _full_spec((1, bw)),
                    pl.BlockSpec((1, B_HEADS, blk, 2 * blk), lambda i: (d, 0, 0, 0)),
                    pl.BlockSpec((1, 1, bw), lambda i: (d, 0, 0)),
                    pl.BlockSpec((1, 1, bw), lambda i: (d, 0, 0)),
                    pl.BlockSpec((1, 1, bw), lambda i: (d, 0, 0))] + [row] * len(extra_in)
        scratch = [pltpu.VMEM((tm + 2 * SUBLANES, bw), F32), pltpu.VMEM((tm, bw), F32),
                   pltpu.VMEM((tm, bw), F32), pltpu.VMEM((1, bw), F32)]
        if reverse:
            scratch.append(pltpu.VMEM((tm, bw), F32))
        return pl.pallas_call(
            functools.partial(_rglru_kernel, reverse=reverse, n_tiles=n_tiles, n_lat_tiles=n_lat_tiles),
            grid=(n_tiles,),
            in_specs=in_specs,
            out_specs=row,
            out_shape=jax.ShapeDtypeStruct((m, bw), BF16 if reverse else F32),
            scratch_shapes=scratch,
            name="rglru_reverse" if reverse else "rglru_forward",
            compiler_params=_params("arbitrary"),
        )(xr, xr, xr, conv_w, conv_b.reshape(1, bw), wcat, rgate_b.reshape(2, 1, bw),
          igate_b.reshape(2, 1, bw), lam.reshape(2, 1, bw), *extra_in)

    hf = run(False, ())
    return run(True, (hf, gr))


def _post_mixer_kernel(*refs, n_att):
    att_refs = refs[:n_att]
    wout_refs = refs[n_att:2 * n_att]
    (x_ref, mod_ref, g2_ref, rwh_ref, rwl_ref, rb_ref,
     x1_ref, h2_ref, eidx_ref, rank_ref, gate_ref, cnt_ref, cnt_scr) = refs[2 * n_att:]
    tm = x_ref.shape[0]

    @pl.when(pl.program_id(0) == 0)
    def _():
        cnt_scr[...] = jnp.zeros(cnt_scr.shape, F32)

    o = _dot(att_refs[0][...], wout_refs[0][...])
    for a_ref, w_ref in zip(att_refs[1:], wout_refs[1:]):
        o = o + _dot(a_ref[...], w_ref[...])
    x1 = x_ref[...] + mod_ref[0, 2:3, :] * o
    x1_ref[...] = x1
    h2 = _modulate(x1, g2_ref[...], mod_ref[0, 3:4, :], mod_ref[0, 4:5, :])
    h2_ref[...] = h2

    hh = h2.astype(BF16)
    hl = (h2 - hh.astype(F32)).astype(BF16)
    both = _dot(hh, rwl_ref[...])
    logits = both[:, :LANES] + _dot(hl, rwh_ref[...]) + both[:, LANES:]
    scores = jax.nn.sigmoid(logits)
    sel = scores + rb_ref[...]
    lane = lax.broadcasted_iota(I32, (tm, LANES), 1)
    picked = jnp.zeros((tm, LANES), jnp.bool_)
    idxs, vals = [], []
    for _ in range(TOP_K):
        mx = jnp.max(sel, axis=-1, keepdims=True)
        idx = jnp.min(jnp.where(sel == mx, lane, LANES), axis=-1, keepdims=True)
        hit = lane == idx
        vals.append(jnp.sum(jnp.where(hit, scores, 0.0), axis=-1, keepdims=True))
        idxs.append(idx)
        sel = jnp.where(hit, -3e38, sel)
        picked = picked | hit
    total = vals[0]
    for v in vals[1:]:
        total = total + v

    pick_f = picked.astype(F32)
    lower = (lax.broadcasted_iota(I32, (tm, tm), 0) > lax.broadcasted_iota(I32, (tm, tm), 1)).astype(BF16)
    rank_dense = _dot(lower, pick_f.astype(BF16)) + cnt_scr[...]
    cnt_scr[...] = cnt_scr[...] + jnp.sum(pick_f, axis=0, keepdims=True)
    cnt_ref[...] = cnt_scr[...]

    eidx = jnp.zeros((tm, LANES), I32)
    rank = jnp.zeros((tm, LANES), I32)
    gate = jnp.zeros((tm, LANES), F32)
    for k in range(TOP_K):
        rk = jnp.sum(jnp.where(lane == idxs[k], rank_dense, 0.0), axis=-1, keepdims=True)
        eidx = jnp.where(lane == k, idxs[k], eidx)
        rank = jnp.where(lane == k, rk.astype(I32), rank)
        gate = jnp.where(lane == k, ROUTED_SCALE * vals[k] / total, gate)
    eidx_ref[...] = eidx
    rank_ref[...] = rank
    gate_ref[...] = gate


def _post_mixer(atts, wouts, x, mod, g2, rwh, rwl, rb, n_lat):
    m, d = x.shape
    tm = ROW_TILE
    n_att = len(atts)
    in_specs = ([_row_spec(tm, a.shape[1]) for a in atts] + [_full_spec(w.shape) for w in wouts]
                + [_row_spec(tm, d), _mod_spec(d, n_lat // tm), _full_spec((1, d)),
                   _full_spec(rwh.shape), _full_spec(rwl.shape), _full_spec((1, LANES))])
    return pl.pallas_call(
        functools.partial(_post_mixer_kernel, n_att=n_att),
        grid=(m // tm,),
        in_specs=in_specs,
        out_specs=[_row_spec(tm, d), _row_spec(tm, d), _row_spec(tm, LANES), _row_spec(tm, LANES),
                   _row_spec(tm, LANES), _full_spec((1, LANES))],
        out_shape=[jax.ShapeDtypeStruct((m, d), F32), jax.ShapeDtypeStruct((m, d), F32),
                   jax.ShapeDtypeStruct((m, LANES), I32), jax.ShapeDtypeStruct((m, LANES), I32),
                   jax.ShapeDtypeStruct((m, LANES), F32), jax.ShapeDtypeStruct((1, LANES), F32)],
        scratch_shapes=[pltpu.VMEM((1, LANES), F32)],
        name="post_mixer",
        compiler_params=_params("arbitrary"),
    )(*atts, *wouts, x, mod, g2.reshape(1, d), rwh, rwl, rb)


def _dispatch_kernel(dest_ref, h2_ref, xs_in, xs_out, sem):
    del xs_in
    tm = h2_ref.shape[0]

    def issue(r, c):
        for k in range(TOP_K):
            pltpu.make_async_copy(h2_ref.at[pl.ds(r, 1)],
                                  xs_out.at[pl.ds(dest_ref[0, 0, r * TOP_K + k], 1)],
                                  sem).start(priority=k % 2)
        return c
    lax.fori_loop(0, tm, issue, 0)
    for _ in range(TOP_K):
        pltpu.make_async_copy(h2_ref, xs_out.at[pl.ds(0, tm)], sem).wait()


def _dispatch(dest, h2, xs):
    m, d = h2.shape
    tm = ROW_TILE
    return pl.pallas_call(
        _dispatch_kernel,
        grid=(m // tm,),
        in_specs=[pl.BlockSpec((1, 1, tm * TOP_K), lambda i: (i, 0, 0), memory_space=pltpu.SMEM),
                  _row_spec(tm, d), pl.BlockSpec(memory_space=pl.ANY)],
        out_specs=pl.BlockSpec(memory_space=pl.ANY),
        out_shape=jax.ShapeDtypeStruct(xs.shape, xs.dtype),
        scratch_shapes=[pltpu.SemaphoreType.DMA],
        input_output_aliases={2: 0},
        name="dispatch",
        compiler_params=_params("arbitrary", disable_bounds_checks=True),
    )(dest.reshape(m // tm, 1, tm * TOP_K), h2, xs)


def _expert_kernel(ps_ref, cnt_ref, xs_hbm, wg_ref, wu_ref, wd_ref, y_hbm,
                   xbuf, ybuf, wgb, wub, wdb, in_sem, out_sem):
    del xs_hbm
    bm = xbuf.shape[1]
    e = pl.program_id(0)
    n_e = pl.num_programs(0)
    base = ps_ref[e]
    nblk = (cnt_ref[e] + bm - 1) // bm
    tail_rows = cnt_ref[e] - (nblk - 1) * bm

    def rows(j):
        return pl.ds(pl.multiple_of(base + j * bm, bm), bm)

    def in_copy(j, slot):
        return pltpu.make_async_copy(y_hbm.at[rows(j)], xbuf.at[slot], in_sem.at[slot])

    def out_copy(j, slot):
        return pltpu.make_async_copy(ybuf.at[slot], y_hbm.at[rows(j)], out_sem.at[slot])

    def first_copy(ex):
        start = pl.multiple_of(ps_ref[ex], bm)
        return pltpu.make_async_copy(y_hbm.at[pl.ds(start, bm)], xbuf.at[0], in_sem.at[0])

    def drain(n):
        for back in (1, 2):
            @pl.when(n >= back)
            def _():
                out_copy(0, (n - back) % 2).wait()

    @pl.when((e == 0) & (nblk > 0))
    def _():
        first_copy(0).start(priority=BLOCK_DMA_QUEUE)

    @pl.when(e == 0)
    def _():
        ybuf[...] = jnp.zeros(ybuf.shape, F32)

    wgb[...] = wg_ref[0, 0].astype(BF16)
    wub[...] = wu_ref[0, 0].astype(BF16)
    wdb[...] = wd_ref[0, 0].astype(BF16)

    @pl.when(e > 0)
    def _():
        drain((cnt_ref[jnp.maximum(e - 1, 0)] + bm - 1) // bm)

    def block(j, slot):
        in_copy(j, slot).wait()

        @pl.when(j + 1 < nblk)
        def _():
            in_copy(j + 1, 1 - slot).start(priority=BLOCK_DMA_QUEUE)

        @pl.when(j >= 2)
        def _():
            out_copy(j - 2, slot).wait()

        def compute(n):
            xb = xbuf[slot, :n].astype(BF16)
            hm = _silu(_dot(xb, wgb[...])) * _dot(xb, wub[...])
            ybuf[slot, :n] = _dot(hm.astype(BF16), wdb[...])

        short = (j == nblk - 1) & (tail_rows <= bm // 2)

        @pl.when(short)
        def _():
            compute(bm // 2)

        @pl.when(jnp.logical_not(short))
        def _():
            compute(bm)

        out_copy(j, slot).start(priority=BLOCK_DMA_QUEUE)

    def pair(jj, c):
        block(2 * jj, 0)

        @pl.when(2 * jj + 1 < nblk)
        def _():
            block(2 * jj + 1, 1)
        return c
    lax.fori_loop(0, (nblk + 1) // 2, pair, 0)

    nxt = jnp.minimum(e + 1, n_e - 1)

    @pl.when((e + 1 < n_e) & (cnt_ref[nxt] > 0))
    def _():
        first_copy(nxt).start(priority=BLOCK_DMA_QUEUE)

    @pl.when(e == n_e - 1)
    def _():
        drain(nblk)


def _experts(xs, pad_start, counts, layer, w_gate, w_up, w_down):
    _, n_e, d, d_e = w_gate.shape
    bm = EXPERT_ROWS
    return pl.pallas_call(
        _expert_kernel,
        grid_spec=pltpu.PrefetchScalarGridSpec(
            num_scalar_prefetch=2,
            grid=(n_e,),
            in_specs=[pl.BlockSpec(memory_space=pl.ANY),
                      pl.BlockSpec((1, 1, d, d_e), lambda e, ps, cn: (layer, e, 0, 0)),
                      pl.BlockSpec((1, 1, d, d_e), lambda e, ps, cn: (layer, e, 0, 0)),
                      pl.BlockSpec((1, 1, d_e, d), lambda e, ps, cn: (layer, e, 0, 0))],
            out_specs=pl.BlockSpec(memory_space=pl.ANY),
            scratch_shapes=[pltpu.VMEM((2, bm, d), F32), pltpu.VMEM((2, bm, d), F32),
                            pltpu.VMEM((d, d_e), BF16), pltpu.VMEM((d, d_e), BF16),
                            pltpu.VMEM((d_e, d), BF16),
                            pltpu.SemaphoreType.DMA((2,)), pltpu.SemaphoreType.DMA((2,))]),
        out_shape=jax.ShapeDtypeStruct(xs.shape, xs.dtype),
        input_output_aliases={2: 0},
        name="experts",
        compiler_params=_params("arbitrary"),
    )(pad_start, counts, xs, w_gate, w_up, w_down)


def _combine_kernel(*refs, final, n_steps):
    if final:
        (dest_ref, next_dest_ref, y_hbm, x1_ref, h2_ref, gate_ref, mod_ref, sg_ref, su_ref, sd_ref,
         fg_ref, o_ref, ybuf, routed_scr, sem) = refs
    else:
        (dest_ref, next_dest_ref, y_hbm, x1_ref, h2_ref, gate_ref, mod_ref, sg_ref, su_ref, sd_ref,
         o_ref, ybuf, routed_scr, sem) = refs
    tc = x1_ref.shape[0]
    i = pl.program_id(0)
    slot = i % 2

    def issue_row(d_ref, r, s):
        for k in range(TOP_K):
            pltpu.make_async_copy(y_hbm.at[pl.ds(d_ref[0, 0, r * TOP_K + k], 1)],
                                  ybuf.at[s, k, pl.ds(r, 1)], sem.at[s]).start(priority=k % 2)

    def wait_tile(s):
        for k in range(TOP_K):
            pltpu.make_async_copy(y_hbm.at[pl.ds(0, tc)], ybuf.at[s, k], sem.at[s]).wait()

    @pl.when(i == 0)
    def _():
        def first(r, c):
            issue_row(dest_ref, r, 0)
            return c
        lax.fori_loop(0, tc, first, 0)

    wait_tile(slot)

    def body(g, c):
        r0 = pl.multiple_of(g * SUBLANES, SUBLANES)
        for rr in range(SUBLANES):
            issue_row(next_dest_ref, r0 + rr, 1 - slot)
        gate = gate_ref[pl.ds(r0, SUBLANES), :]
        acc = gate[:, 0:1] * ybuf[slot, 0, pl.ds(r0, SUBLANES), :]
        for k in range(1, TOP_K):
            acc = acc + gate[:, k:k + 1] * ybuf[slot, k, pl.ds(r0, SUBLANES), :]
        routed_scr[pl.ds(r0, SUBLANES), :] = acc
        return c
    lax.fori_loop(0, tc // SUBLANES, body, 0)

    @pl.when(i == n_steps - 1)
    def _():
        wait_tile(1 - slot)

    hb = h2_ref[...].astype(BF16)
    shared = _dot((_silu(_dot(hb, sg_ref[...])) * _dot(hb, su_ref[...])).astype(BF16), sd_ref[...])
    x2 = x1_ref[...] + mod_ref[0, 5:6, :] * (routed_scr[...] + shared)
    if final:
        ms = jnp.mean(x2 * x2, axis=-1, keepdims=True)
        x2 = x2 * lax.rsqrt(ms + NORM_EPS) * fg_ref[...]
    o_ref[...] = x2


def _combine(dest, y, x1, h2, gate, mod, sg, su, sd, final_g, n_lat):
    m, d = x1.shape
    tc = COMBINE_TILE
    n_steps = m // tc
    final = final_g is not None
    dest3 = dest.reshape(n_steps, 1, tc * TOP_K)
    in_specs = [pl.BlockSpec((1, 1, tc * TOP_K), lambda i: (i, 0, 0), memory_space=pltpu.SMEM),
                pl.BlockSpec((1, 1, tc * TOP_K), lambda i: (jnp.minimum(i + 1, n_steps - 1), 0, 0),
                             memory_space=pltpu.SMEM),
                pl.BlockSpec(memory_space=pl.ANY),
                _row_spec(tc, d), _row_spec(tc, d), _row_spec(tc, LANES), _mod_spec(d, n_lat // tc),
                _full_spec(sg.shape), _full_spec(su.shape), _full_spec(sd.shape)]
    args = [dest3, dest3, y, x1, h2, gate, mod, sg, su, sd]
    if final:
        in_specs.append(_full_spec((1, d)))
        args.append(final_g.reshape(1, d))
    return pl.pallas_call(
        functools.partial(_combine_kernel, final=final, n_steps=n_steps),
        grid=(n_steps,),
        in_specs=in_specs,
        out_specs=_row_spec(tc, d),
        out_shape=jax.ShapeDtypeStruct((m, d), F32),
        scratch_shapes=[pltpu.VMEM((2, TOP_K, tc, d), F32), pltpu.VMEM((tc, d), F32),
                        pltpu.SemaphoreType.DMA((2,))],
        name="combine",
        compiler_params=_params("arbitrary", disable_bounds_checks=True),
    )(*args)


def _sorted_rows(m, n_experts, bm):
    return -(-(m * TOP_K + n_experts * (bm - 1)) // bm) * bm


def _routing_tables(eidx, rank, counts, bm):
    padded = (counts + bm - 1) // bm * bm
    pad_start = jnp.cumsum(padded) - padded
    experts = jnp.arange(counts.shape[0], dtype=I32)
    start_of_pick = jnp.sum(jnp.where(eidx[:, :, None] == experts, pad_start, 0), axis=-1)
    return pad_start.astype(I32), (start_of_pick + rank).astype(I32)


def kernel(x, c, ctx, c_ctx, w_mod, b_mod, norm1_g, norm2_g, final_norm_g, a_w_in, a_w_out, a_q_norm,
           a_k_norm, b_conv_w, b_conv_b, b_rgate_w, b_rgate_b, b_igate_w, b_igate_b, b_lambda, c_w_in,
           c_w_out, c_sink, moe_router_w, moe_router_b, moe_w_gate, moe_w_up, moe_w_down,
           moe_shared_gate, moe_shared_up, moe_shared_down):
    batch, n_lat, d = x.shape
    n_ctx = ctx.shape[1]
    depth = w_mod.shape[0]
    n_experts = moe_router_w.shape[2]
    assert batch == 1 and n_ctx == ROW_TILE and n_lat % ROW_TILE == 0
    assert n_lat >= WIN_Q_TILE + 2 * WINDOW and n_experts <= LANES

    tk = min(ATT_KV_CHUNK, n_lat)
    xs = jnp.concatenate([x[0], ctx[0]], axis=0)
    mods = _ada_params(jnp.stack([c[0], c_ctx]), w_mod, b_mod).reshape(depth, 2, N_MOD, d)
    tables_a = _rope_tables(n_lat, n_ctx, A_HEAD_DIM)
    tables_c = _rope_tables(n_lat, n_ctx, C_HEAD_DIM)

    pad_e = LANES - n_experts
    rw = jnp.pad(moe_router_w, ((0, 0), (0, 0), (0, pad_e)))
    rwh = rw.astype(BF16)
    rwl = jnp.concatenate([rwh, (rw - rwh.astype(F32)).astype(BF16)], axis=-1)
    rb = jnp.pad(moe_router_b, ((0, 0), (0, pad_e)), constant_values=NEG_INF).reshape(depth, 1, LANES)
    sorted_x = jnp.zeros((_sorted_rows(n_lat + n_ctx, n_experts, EXPERT_ROWS), d), F32)

    for layer in range(depth):
        i = layer // 2
        mod = mods[layer]
        if layer % 2 == 0:
            q, k, v, xr, gr = _in_proj_even(xs, norm1_g[layer], mod, a_w_in[i].astype(BF16), tables_a,
                                            a_q_norm[i], a_k_norm[i], n_lat)
            att = _dense_attention(q, k, v, n_lat, tk)
            wcat = jnp.concatenate([b_rgate_w[i], b_igate_w[i]], axis=-1).astype(BF16)
            rec = _rglru(xr, gr, b_conv_w[i], b_conv_b[i], wcat, b_rgate_b[i], b_igate_b[i],
                         b_lambda[i], n_lat)
            w_out = a_w_out[i].astype(BF16)
            a_q = att.shape[1]
            atts, wouts = (att, rec), (w_out[:a_q], w_out[a_q:])
        else:
            q, qs, k, v = _in_proj_odd(xs, norm1_g[layer], mod, c_w_in[i].astype(BF16), tables_c, n_lat)
            att = _window_attention(q, qs, k, v, c_sink[i], n_lat)
            atts, wouts = (att,), (c_w_out[i].astype(BF16),)
        x1, h2, eidx, rank, gate, cnt = _post_mixer(atts, wouts, xs, mod, norm2_g[layer],
                                                    rwh[layer], rwl[layer], rb[layer], n_lat)
        counts = cnt[0, :n_experts].astype(I32)
        pad_start, dest = _routing_tables(eidx[:, :TOP_K], rank[:, :TOP_K], counts, EXPERT_ROWS)
        sorted_x = _dispatch(dest, h2, sorted_x)
        sorted_x = _experts(sorted_x, pad_start, counts, layer, moe_w_gate, moe_w_up, moe_w_down)
        xs = _combine(dest, sorted_x, x1, h2, gate, mod, moe_shared_gate[layer].astype(BF16),
                      moe_shared_up[layer].astype(BF16), moe_shared_down[layer].astype(BF16),
                      final_norm_g if layer == depth - 1 else None, n_lat)
    return xs[:n_lat].reshape(batch, n_lat, d)
```
